```python
import math
import jax, jax.numpy as jnp
from jax import lax
import numpy as np

D_MODEL = 2048
BATCH = 4
SEQ = 4096
DEPTH = 4

N_A_LAYERS = DEPTH // 2
N_B_LAYERS = DEPTH - N_A_LAYERS

CONV_WIDTH = 3

BRANCHES = ((128, 1), (512, 4), (2048, 16))
N_BRANCH = len(BRANCHES)
HEADS_PER_BRANCH = 8
HEAD_DIM = 128
Q_WIDTH = N_BRANCH * HEADS_PER_BRANCH * HEAD_DIM
ATT_OUT_WIDTH = HEADS_PER_BRANCH * HEAD_DIM
ROT_DIM = HEAD_DIM // 4
ROPE_THETA = 500000.0

N_EXPERTS = 16
N_GROUPS = 4
EXPERTS_PER_GROUP = N_EXPERTS // N_GROUPS
TOP_K = 2
D_FF_EXPERT = 1408
MOE_BLOCK = 128

DEEPNORM_ALPHA = (2.0 * DEPTH) ** 0.25
DEEPNORM_BETA = (8.0 * DEPTH) ** -0.25
LN_EPS = 1e-5

kernel_name = "hybrid_yoco_shortconv_dilated_attn_grouped_moe"


def layer_norm(x, g, b):
    xf = x.astype(jnp.float32)
    mean = jnp.mean(xf, axis=-1, keepdims=True)
    var = jnp.mean(jnp.square(xf - mean), axis=-1, keepdims=True)
    y = (xf - mean) * lax.rsqrt(var + LN_EPS)
    return (y * g.astype(jnp.float32) + b.astype(jnp.float32)).astype(x.dtype)


def rope_tables(seq_len):
    inv_freq = ROPE_THETA ** (-jnp.arange(0, ROT_DIM, 2, dtype=jnp.float32) / ROT_DIM)
    ang = jnp.arange(seq_len, dtype=jnp.float32)[:, None] * inv_freq[None, :]
    return jnp.cos(ang), jnp.sin(ang)


def partial_rope(t, cos, sin):
    shape = (cos.shape[0],) + (1,) * (t.ndim - 3) + (cos.shape[1],)
    c = cos.reshape(shape).astype(t.dtype)
    s = sin.reshape(shape).astype(t.dtype)
    half = ROT_DIM // 2
    t1, t2, rest = t[..., :half], t[..., half:ROT_DIM], t[..., ROT_DIM:]
    return jnp.concatenate([t1 * c - t2 * s, t2 * c + t1 * s, rest], axis=-1)


def short_conv_mixer(x, w_in, conv_w, w_out):
    d = x.shape[-1]
    gate_b, gate_c, h = jnp.split(x @ w_in, 3, axis=-1)
    u = gate_c * h
    conv = lax.conv_general_dilated(
        u, conv_w[:, None, :], window_strides=(1,), padding=[(CONV_WIDTH - 1, 0)],
        dimension_numbers=("NWC", "WIO", "NWC"), feature_group_count=d)
    return (gate_b * conv) @ w_out


def dilated_local_attention(q, k, v, dilation, span):
    bsz, seq, nh, dh = q.shape
    sub_len = seq // dilation
    n_blk = -(-sub_len // span)
    pad = n_blk * span - sub_len

    def to_blocks(t):
        t = t.reshape(bsz, sub_len, dilation, nh, dh)
        t = jnp.pad(t, ((0, 0), (0, pad), (0, 0), (0, 0), (0, 0)))
        return t.reshape(bsz, n_blk, span, dilation, nh, dh)

    qb, kb, vb = to_blocks(q), to_blocks(k), to_blocks(v)
    zpad = ((0, 0), (1, 0), (0, 0), (0, 0), (0, 0), (0, 0))
    kk = jnp.concatenate([jnp.pad(kb[:, :-1], zpad), kb], axis=2)
    vv = jnp.concatenate([jnp.pad(vb[:, :-1], zpad), vb], axis=2)

    scale = 1.0 / math.sqrt(dh)
    s = jnp.einsum("bnqrhd,bnkrhd->bnrhqk", qb, kk, preferred_element_type=jnp.float32) * scale
    qpos = span + jnp.arange(span)
    kpos = jnp.arange(2 * span)
    dist = qpos[:, None] - kpos[None, :]
    band = (dist >= 0) & (dist <= span)
    has_prev = (jnp.arange(n_blk)[:, None, None] > 0) | (kpos[None, None, :] >= span)
    mask = band[None] & has_prev
    s = jnp.where(mask[None, :, None, None], s, -jnp.inf)

    m = jnp.max(s, axis=-1, keepdims=True)
    p = jnp.exp(s - m)
    den = jnp.sum(p, axis=-1)
    o = jnp.einsum("bnrhqk,bnkrhd->bnrhqd", p.astype(v.dtype), vv,
                   preferred_element_type=jnp.float32) / den[..., None]
    lse = m[..., 0] + jnp.log(den)

    o = jnp.transpose(o, (0, 1, 4, 2, 3, 5)).reshape(bsz, n_blk * span, dilation, nh, dh)
    lse = jnp.transpose(lse, (0, 1, 4, 2, 3)).reshape(bsz, n_blk * span, dilation, nh)
    o = o[:, :sub_len].reshape(bsz, seq, nh, dh).astype(q.dtype)
    lse = lse[:, :sub_len].reshape(bsz, seq, nh)
    return o, lse


def shared_kv(x, w_kv, cos, sin):
    bsz, seq, _ = x.shape
    kv = (x @ w_kv).reshape(bsz, seq, 2, N_BRANCH, HEADS_PER_BRANCH, HEAD_DIM)
    return partial_rope(kv[:, :, 0], cos, sin), kv[:, :, 1]


def dilated_attention_mixer(x, w_q, w_o, k_sh, v_sh, cos, sin):
    bsz, seq, _ = x.shape
    q = (x @ w_q).reshape(bsz, seq, N_BRANCH, HEADS_PER_BRANCH, HEAD_DIM)
    q = partial_rope(q, cos, sin)
    outs, lses = [], []
    for g, (window, dilation) in enumerate(BRANCHES):
        o, l = dilated_local_attention(q[:, :, g], k_sh[:, :, g], v_sh[:, :, g],
                                       dilation, window // dilation)
        outs.append(o)
        lses.append(l)
    o = jnp.stack(outs, axis=2)
    wts = jax.nn.softmax(jnp.stack(lses, axis=2), axis=2)
    merged = jnp.sum(o * wts[..., None].astype(o.dtype), axis=2)
    return merged.reshape(bsz, seq, ATT_OUT_WIDTH) @ w_o


def route(x2, router_w, router_bias):
    logits = x2.astype(jnp.float32) @ router_w.astype(jnp.float32)
    scores = jax.nn.sigmoid(logits)
    sel = scores + router_bias.astype(jnp.float32)
    group_score = jnp.sum(lax.top_k(sel.reshape(-1, N_GROUPS, EXPERTS_PER_GROUP), 2)[0], axis=-1)
    _, gidx = lax.top_k(group_score, 1)
    in_group = (jnp.arange(N_EXPERTS) // EXPERTS_PER_GROUP)[None, :] == gidx
    _, eidx = lax.top_k(jnp.where(in_group, sel, -jnp.inf), TOP_K)
    g = jnp.take_along_axis(scores, eidx, axis=-1)
    return eidx, g / jnp.sum(g, axis=-1, keepdims=True)


def routed_moe(x, router_w, router_bias, w1, w3, w2):
    bsz, seq, d = x.shape
    x2 = x.reshape(-1, d)
    n_tok = x2.shape[0]
    eidx, gates = route(x2, router_w, router_bias)
    n_assign = n_tok * TOP_K
    flat_e = eidx.reshape(-1)
    flat_tok = jnp.repeat(jnp.arange(n_tok, dtype=jnp.int32), TOP_K)
    flat_w = gates.reshape(-1).astype(x.dtype)

    order = jnp.argsort(flat_e)
    sorted_e, sorted_tok, sorted_w = flat_e[order], flat_tok[order], flat_w[order]
    counts = jnp.bincount(flat_e, length=N_EXPERTS)
    starts = jnp.cumsum(counts) - counts
    padded = ((counts + MOE_BLOCK - 1) // MOE_BLOCK) * MOE_BLOCK
    pad_end = jnp.cumsum(padded)
    pad_start = pad_end - padded
    dest = pad_start[sorted_e] + (jnp.arange(n_assign) - starts[sorted_e])

    n_blocks = -(-n_assign // MOE_BLOCK) + N_EXPERTS
    m_pad = n_blocks * MOE_BLOCK
    row_tok = jnp.zeros((m_pad,), jnp.int32).at[dest].set(sorted_tok)
    row_w = jnp.zeros((m_pad,), x.dtype).at[dest].set(sorted_w)
    block_e = jnp.clip(jnp.searchsorted(pad_end, jnp.arange(n_blocks) * MOE_BLOCK, side="right"),
                       0, N_EXPERTS - 1)
    xs = x2[row_tok].reshape(n_blocks, MOE_BLOCK, d)

    def expert_block(args):
        xb, e = args
        h = jax.nn.silu(xb @ w1[e]) * (xb @ w3[e])
        return h @ w2[e]

    ys = lax.map(expert_block, (xs, block_e)).reshape(m_pad, d)
    out = jax.ops.segment_sum(ys * row_w[:, None], row_tok, num_segments=n_tok)
    return out.reshape(bsz, seq, d).astype(x.dtype)


def setup_inputs(seed: int = 0) -> dict:
    key = jax.random.key(seed)
    ks = jax.random.split(key, 20)
    d = D_MODEL
    f32 = jnp.float32
    nrm = lambda k, shape, scale: jax.random.normal(k, shape, f32) * scale
    kv_w = nrm(ks[4], (d, 2 * Q_WIDTH), d ** -0.5)
    kv_w = kv_w * jnp.concatenate([jnp.ones((Q_WIDTH,), f32), jnp.full((Q_WIDTH,), DEEPNORM_BETA, f32)])
    return {
        "x": nrm(ks[0], (BATCH, SEQ, d), 1.0),
        "a_w_in": nrm(ks[1], (N_A_LAYERS, d, 3 * d), d ** -0.5),
        "a_conv_w": nrm(ks[2], (N_A_LAYERS, CONV_WIDTH, d), CONV_WIDTH ** -0.5),
        "a_w_out": nrm(ks[3], (N_A_LAYERS, d, d), DEEPNORM_BETA * d ** -0.5),
        "kv_w": kv_w,
        "b_w_q": nrm(ks[5], (N_B_LAYERS, d, Q_WIDTH), d ** -0.5),
        "b_w_o": nrm(ks[6], (N_B_LAYERS, ATT_OUT_WIDTH, d), DEEPNORM_BETA * ATT_OUT_WIDTH ** -0.5),
        "router_w": nrm(ks[7], (d, N_EXPERTS), d ** -0.5),
        "router_bias": nrm(ks[8], (N_EXPERTS,), 0.01),
        "moe_w1": nrm(ks[9], (DEPTH, N_EXPERTS, d, D_FF_EXPERT), d ** -0.5),
        "moe_w3": nrm(ks[10], (DEPTH, N_EXPERTS, d, D_FF_EXPERT), d ** -0.5),
        "moe_w2": nrm(ks[11], (DEPTH, N_EXPERTS, D_FF_EXPERT, d), DEEPNORM_BETA * D_FF_EXPERT ** -0.5),
        "ln1_g": 1.0 + nrm(ks[12], (DEPTH, d), 0.01),
        "ln1_b": nrm(ks[13], (DEPTH, d), 0.01),
        "ln2_g": 1.0 + nrm(ks[14], (DEPTH, d), 0.01),
        "ln2_b": nrm(ks[15], (DEPTH, d), 0.01),
    }


def reference(x, a_w_in, a_conv_w, a_w_out, kv_w, b_w_q, b_w_o, router_w, router_bias,
              moe_w1, moe_w3, moe_w2, ln1_g, ln1_b, ln2_g, ln2_b):
    cos, sin = rope_tables(x.shape[1])
    k_sh, v_sh = None, None
    for i in range(DEPTH):
        if i < N_A_LAYERS:
            mix = short_conv_mixer(x, a_w_in[i], a_conv_w[i], a_w_out[i])
        else:
            if i == N_A_LAYERS:
                k_sh, v_sh = shared_kv(x, kv_w, cos, sin)
            j = i - N_A_LAYERS
            mix = dilated_attention_mixer(x, b_w_q[j], b_w_o[j], k_sh, v_sh, cos, sin)
        x = layer_norm(DEEPNORM_ALPHA * x + mix, ln1_g[i], ln1_b[i])
        ffn = routed_moe(x, router_w, router_bias, moe_w1[i], moe_w3[i], moe_w2[i])
        x = layer_norm(DEEPNORM_ALPHA * x + ffn, ln2_g[i], ln2_b[i])
    return x
```

```python
import functools
import math

import jax
import jax.numpy as jnp
from jax import lax
from jax.experimental import pallas as pl
from jax.experimental.pallas import tpu as pltpu

D_MODEL = 2048
BATCH = 4
SEQ = 4096
DEPTH = 4
N_TOK = BATCH * SEQ
N_A_LAYERS = DEPTH // 2
CONV_WIDTH = 3
BRANCHES = ((128, 1), (512, 4), (2048, 16))
N_BRANCH = len(BRANCHES)
HEADS_PER_BRANCH = 8
HEAD_DIM = 128
BRANCH_WIDTH = HEADS_PER_BRANCH * HEAD_DIM
SPAN = 128
ROT_DIM = HEAD_DIM // 4
ROPE_THETA = 500000.0
N_EXPERTS = 16
N_GROUPS = 4
EXPERTS_PER_GROUP = N_EXPERTS // N_GROUPS
TOP_K = 2
D_FF = 1408
DEEPNORM_ALPHA = (2.0 * DEPTH) ** 0.25
LN_EPS = 1e-5

F32 = jnp.float32
BF16 = jnp.bfloat16

VMEM_LIMIT_BYTES = 56 * 1024 * 1024
LANES = 128

CONV_TM, CONV_TN = 1024, 256
PROJ_TM = 512
OUT_TM = 256
ATTN_QB = 256
MOE_BLK = 1024
MOE_TF = 256
MOE_NF = -(-D_FF // MOE_TF)
MOE_NBLK = (N_TOK * TOP_K) // MOE_BLK + N_EXPERTS
MOE_ROWS = MOE_NBLK * MOE_BLK
COMB_TM = 256


def _params(n_axes):
    return pltpu.CompilerParams(
        dimension_semantics=("arbitrary",) * n_axes,
        vmem_limit_bytes=VMEM_LIMIT_BYTES,
    )


def _conv_proj_kernel(x_ref, wb_ref, wc_ref, wh_ref, cw_ref, y_ref, wbf_ref, ext_ref):
    i = pl.program_id(1)
    tm = x_ref.shape[0]

    @pl.when(i == 0)
    def _():
        wbf_ref[0] = wb_ref[...].astype(BF16)
        wbf_ref[1] = wc_ref[...].astype(BF16)
        wbf_ref[2] = wh_ref[...].astype(BF16)

    @pl.when(i % (SEQ // tm) == 0)
    def _():
        ext_ref[0:8, :] = jnp.zeros((8, ext_ref.shape[1]), F32)

    x = x_ref[...]
    gate_b = jnp.dot(x, wbf_ref[0], preferred_element_type=F32)
    gate_c = jnp.dot(x, wbf_ref[1], preferred_element_type=F32)
    h = jnp.dot(x, wbf_ref[2], preferred_element_type=F32)
    u = gate_c * h
    ext_ref[8:8 + tm, :] = u
    u1 = ext_ref[7:7 + tm, :]
    u2 = ext_ref[6:6 + tm, :]
    cw = cw_ref[...]
    conv = cw[2:3, :] * u + cw[1:2, :] * u1 + cw[0:1, :] * u2
    y_ref[...] = (gate_b * conv).astype(BF16)
    ext_ref[0:8, :] = ext_ref[tm:tm + 8, :]


def _conv_proj(xb, w_in, conv_w, layer):
    tm, tn = CONV_TM, CONV_TN
    nj = D_MODEL // tn
    wspec = lambda off: pl.BlockSpec((None, D_MODEL, tn), lambda j, i: (layer, 0, j + off * nj))
    return pl.pallas_call(
        _conv_proj_kernel,
        out_shape=jax.ShapeDtypeStruct((N_TOK, D_MODEL), BF16),
        grid=(nj, N_TOK // tm),
        in_specs=[
            pl.BlockSpec((tm, D_MODEL), lambda j, i: (i, 0)),
            wspec(0), wspec(1), wspec(2),
            pl.BlockSpec((None, CONV_WIDTH, tn), lambda j, i: (layer, 0, j)),
        ],
        out_specs=pl.BlockSpec((tm, tn), lambda j, i: (i, j)),
        scratch_shapes=[pltpu.VMEM((3, D_MODEL, tn), BF16), pltpu.VMEM((tm + 8, tn), F32)],
        compiler_params=_params(2),
        name="conv_proj",
    )(xb, w_in, w_in, w_in, conv_w)


def _proj_kernel(x_ref, w_ref, c_ref, s1_ref, s2_ref, o_ref, wbf_ref, *, rope):
    i = pl.program_id(1)

    @pl.when(i == 0)
    def _():
        wbf_ref[...] = w_ref[...].astype(BF16)

    acc = jnp.dot(x_ref[...], wbf_ref[...], preferred_element_type=F32)
    if not rope:
        o_ref[...] = acc.astype(BF16)
        return
    c = c_ref[...]
    s1 = s1_ref[...]
    s2 = s2_ref[...]
    for hd in range(HEADS_PER_BRANCH):
        sl = slice(hd * HEAD_DIM, (hd + 1) * HEAD_DIM)
        t = acc[:, sl]
        r = t * c + pltpu.roll(t, HEAD_DIM - ROT_DIM // 2, 1) * s1 + pltpu.roll(t, ROT_DIM // 2, 1) * s2
        o_ref[:, sl] = r.astype(BF16)


def _proj(xb, w3d, layer, col0, ncol, tables, rope):
    tm = PROJ_TM
    tn = BRANCH_WIDTH
    tab = pl.BlockSpec((tm, HEAD_DIM), lambda j, i: (i % (SEQ // tm), 0))
    return pl.pallas_call(
        functools.partial(_proj_kernel, rope=rope),
        out_shape=jax.ShapeDtypeStruct((ncol, N_TOK, tn), BF16),
        grid=(ncol, N_TOK // tm),
        in_specs=[
            pl.BlockSpec((tm, D_MODEL), lambda j, i: (i, 0)),
            pl.BlockSpec((None, D_MODEL, tn), lambda j, i: (layer, 0, j + col0)),
            tab, tab, tab,
        ],
        out_specs=pl.BlockSpec((None, tm, tn), lambda j, i: (j, i, 0)),
        scratch_shapes=[pltpu.VMEM((D_MODEL, tn), BF16)],
        compiler_params=_params(2),
        name="proj_rope" if rope else "proj",
    )(xb, w3d, *tables)


def _rope_tables(scale):
    inv_freq = ROPE_THETA ** (-jnp.arange(0, ROT_DIM, 2, dtype=F32) / ROT_DIM)
    ang = jnp.arange(SEQ, dtype=F32)[:, None] * inv_freq[None, :]
    cos, sin = jnp.cos(ang), jnp.sin(ang)
    half = ROT_DIM // 2
    rest = HEAD_DIM - ROT_DIM
    c = jnp.concatenate([cos, cos, jnp.ones((SEQ, rest), F32)], axis=1) * scale
    s1 = jnp.concatenate([-sin, jnp.zeros((SEQ, HEAD_DIM - half), F32)], axis=1) * scale
    s2 = jnp.concatenate([jnp.zeros((SEQ, half), F32), sin, jnp.zeros((SEQ, rest), F32)], axis=1) * scale
    return c, s1, s2


def _attn_kernel(q_ref, kc_ref, kp_ref, vc_ref, vp_ref, o_ref, l_ref):
    qb = q_ref.shape[0]
    qi = lax.broadcasted_iota(jnp.int32, (SPAN, 2 * SPAN), 0)
    kj = lax.broadcasted_iota(jnp.int32, (SPAN, 2 * SPAN), 1)
    dist = SPAN + qi - kj
    band = (dist >= 0) & (dist <= SPAN)
    first_key = jnp.where(pl.program_id(1) == 0, SPAN, 0)
    band_first = band & (kj >= first_key)
    for a in range(qb // SPAN):
        rows = slice(a * SPAN, (a + 1) * SPAN)
        for hd in range(HEADS_PER_BRANCH):
            cols = slice(hd * HEAD_DIM, (hd + 1) * HEAD_DIM)
            q = q_ref[rows, cols]
            if a == 0:
                k_prev, v_prev = kp_ref[:, cols], vp_ref[:, cols]
                mask = band_first
            else:
                prows = slice((a - 1) * SPAN, a * SPAN)
                k_prev, v_prev = kc_ref[prows, cols], vc_ref[prows, cols]
                mask = band
            kk = jnp.concatenate([k_prev, kc_ref[rows, cols]], axis=0)
            vv = jnp.concatenate([v_prev, vc_ref[rows, cols]], axis=0)
            s = lax.dot_general(q, kk, (((1,), (1,)), ((), ())), preferred_element_type=F32)
            s = jnp.where(mask, s, -jnp.inf)
            m = jnp.max(s, axis=-1, keepdims=True)
            p = jnp.exp(s - m)
            den = jnp.sum(p, axis=-1, keepdims=True)
            o = jnp.dot(p.astype(BF16), vv, preferred_element_type=F32) / den
            o_ref[rows, cols] = o.astype(BF16)
            l_ref[rows, cols] = jnp.broadcast_to(m + jnp.log(den), (SPAN, HEAD_DIM))


def _attn_branch(q, k, v, dilation):
    sub = SEQ // dilation
    qb = min(ATTN_QB, sub)
    w = BRANCH_WIDTH
    view = lambda t: t.reshape(BATCH, sub, dilation * w)
    cur = pl.BlockSpec((None, qb, w), lambda b, n, r: (b, n, r))
    prev = pl.BlockSpec((None, SPAN, w),
                        lambda b, n, r: (b, jnp.maximum(n * (qb // SPAN) - 1, 0), r))
    o, l = pl.pallas_call(
        _attn_kernel,
        out_shape=(jax.ShapeDtypeStruct((BATCH, sub, dilation * w), BF16),
                   jax.ShapeDtypeStruct((BATCH, sub, dilation * w), F32)),
        grid=(BATCH, sub // qb, dilation),
        in_specs=[cur, cur, prev, cur, prev],
        out_specs=(cur, cur),
        compiler_params=_params(3),
        name=f"dilated_attn_d{dilation}",
    )(view(q), view(k), view(k), view(v), view(v))
    return o.reshape(N_TOK, w), l.reshape(N_TOK, w)


def _layer_norm(z, g, b):
    mean = jnp.mean(z, axis=-1, keepdims=True)
    zc = z - mean
    var = jnp.mean(zc * zc, axis=-1, keepdims=True)
    return zc * lax.rsqrt(var + LN_EPS) * g + b


def _route(logits_t, bias_col):
    scores = jax.nn.sigmoid(logits_t)
    sel = scores + bias_col
    sel_r = [sel[e:e + 1, :] for e in range(N_EXPERTS)]
    sc_r = [scores[e:e + 1, :] for e in range(N_EXPERTS)]
    best_g = None
    for g in range(N_GROUPS):
        v = sel_r[g * EXPERTS_PER_GROUP:(g + 1) * EXPERTS_PER_GROUP]
        gs = None
        for a in range(EXPERTS_PER_GROUP):
            for b in range(a + 1, EXPERTS_PER_GROUP):
                ps = v[a] + v[b]
                gs = ps if gs is None else jnp.maximum(gs, ps)
        if best_g is None:
            best_g, gidx = gs, jnp.zeros_like(gs, dtype=jnp.int32)
        else:
            upd = gs > best_g
            best_g = jnp.where(upd, gs, best_g)
            gidx = jnp.where(upd, g, gidx)
    cand, raw = [], []
    for j in range(EXPERTS_PER_GROUP):
        cv, rv = sel_r[j], sc_r[j]
        for g in range(1, N_GROUPS):
            pick = gidx == g
            cv = jnp.where(pick, sel_r[g * EXPERTS_PER_GROUP + j], cv)
            rv = jnp.where(pick, sc_r[g * EXPERTS_PER_GROUP + j], rv)
        cand.append(cv)
        raw.append(rv)

    def argmax4(vals):
        best, idx = vals[0], jnp.zeros_like(gidx)
        for j in range(1, EXPERTS_PER_GROUP):
            upd = vals[j] > best
            best = jnp.where(upd, vals[j], best)
            idx = jnp.where(upd, j, idx)
        return idx

    i1 = argmax4(cand)
    i2 = argmax4([jnp.where(i1 == j, -jnp.inf, cand[j]) for j in range(EXPERTS_PER_GROUP)])
    pick_raw = lambda idx: sum(jnp.where(idx == j, raw[j], 0.0) for j in range(EXPERTS_PER_GROUP))
    g1, g2 = pick_raw(i1), pick_raw(i2)
    tot = g1 + g2
    e1 = gidx * EXPERTS_PER_GROUP + i1
    e2 = gidx * EXPERTS_PER_GROUP + i2
    return e1, e2, g1 / tot, g2 / tot


def _out_ln_route_tail(mix, x_ref, g_ref, b_ref, rw_ref, rb_ref, x1_ref, e_ref, gt_ref):
    z = DEEPNORM_ALPHA * x_ref[...] + mix
    x1 = _layer_norm(z, g_ref[...], b_ref[...])
    x1_ref[...] = x1
    logits = jnp.dot(x1, rw_ref[...], precision=lax.Precision.HIGHEST, preferred_element_type=F32)
    logits_t = logits.T[0:N_EXPERTS, :]
    e1, e2, g1, g2 = _route(logits_t, rb_ref[...])
    e_ref[0:1, :] = e1
    e_ref[1:2, :] = e2
    gt_ref[0:1, :] = g1
    gt_ref[1:2, :] = g2


def _out_ln_route_a_kernel(a_ref, w_ref, *rest):
    mix = jnp.dot(a_ref[...], w_ref[...], preferred_element_type=F32)
    _out_ln_route_tail(mix, *rest)


def _out_ln_route_b_kernel(o0_ref, o1_ref, o2_ref, l0_ref, l1_ref, l2_ref, w_ref, *rest):
    l0, l1, l2 = l0_ref[...], l1_ref[...], l2_ref[...]
    lm = jnp.maximum(jnp.maximum(l0, l1), l2)
    w0, w1, w2 = jnp.exp(l0 - lm), jnp.exp(l1 - lm), jnp.exp(l2 - lm)
    merged = (o0_ref[...].astype(F32) * w0 + o1_ref[...].astype(F32) * w1
              + o2_ref[...].astype(F32) * w2) / (w0 + w1 + w2)
    mix = jnp.dot(merged.astype(BF16), w_ref[...], preferred_element_type=F32)
    _out_ln_route_tail(mix, *rest)


def _out_ln_route(lhs, w_bf, x, ln_g, ln_b, rw_pad, rb_col):
    tm = OUT_TM
    kdim = w_bf.shape[0]
    row = lambda width: pl.BlockSpec((tm, width), lambda i: (i, 0))
    full = lambda shape: pl.BlockSpec(shape, lambda i: (0,) * len(shape))
    body = _out_ln_route_a_kernel if len(lhs) == 1 else _out_ln_route_b_kernel
    lane_row = pl.BlockSpec((TOP_K, tm), lambda i: (0, i))
    return pl.pallas_call(
        body,
        out_shape=(jax.ShapeDtypeStruct((N_TOK, D_MODEL), F32),
                   jax.ShapeDtypeStruct((TOP_K, N_TOK), jnp.int32),
                   jax.ShapeDtypeStruct((TOP_K, N_TOK), F32)),
        grid=(N_TOK // tm,),
        in_specs=[row(kdim)] * len(lhs) + [
            full((kdim, D_MODEL)), row(D_MODEL), full((1, D_MODEL)), full((1, D_MODEL)),
            full((D_MODEL, LANES)), full((N_EXPERTS, 1)),
        ],
        out_specs=(row(D_MODEL), lane_row, lane_row),
        compiler_params=_params(1),
        name="out_ln_route_conv" if len(lhs) == 1 else "out_ln_route_attn",
    )(*lhs, w_bf, x, ln_g.reshape(1, D_MODEL), ln_b.reshape(1, D_MODEL), rw_pad, rb_col)


def _build_plan(eidx, gates):
    flat_e = eidx.reshape(-1)
    onehot = (flat_e[:, None] == jnp.arange(N_EXPERTS, dtype=jnp.int32)[None, :]).astype(jnp.int32)
    csum = jnp.cumsum(onehot, axis=0)
    rank = jnp.sum(onehot * csum, axis=1) - 1
    counts = csum[-1]
    nblk_e = (counts + MOE_BLK - 1) // MOE_BLK
    blk_end = jnp.cumsum(nblk_e)
    blk_start = blk_end - nblk_e
    dest = (blk_start[flat_e] * MOE_BLK + rank).astype(jnp.int32)
    n_valid = blk_end[-1:].astype(jnp.int32)
    block_e = jnp.clip(jnp.searchsorted(blk_end, jnp.arange(MOE_NBLK, dtype=jnp.int32), side="right"),
                       0, N_EXPERTS - 1).astype(jnp.int32)
    tok = jnp.tile(jnp.arange(N_TOK, dtype=jnp.int32), TOP_K)
    row_tok = jnp.zeros((MOE_ROWS,), jnp.int32).at[dest].set(tok)
    return row_tok, block_e, n_valid, dest[:N_TOK], dest[N_TOK:], gates.T


def _gather_kernel(tok_ref, nv_ref, x_hbm, out_ref, buf_ref, sem):
    i = pl.program_id(0)
    blk = out_ref.shape[0]

    @pl.when(i < nv_ref[0])
    def _():
        def start(r, carry):
            t = tok_ref[i * blk + r]
            pltpu.make_async_copy(x_hbm.at[pl.ds(t, 1)], buf_ref.at[pl.ds(r, 1)], sem).start()
            return carry

        lax.fori_loop(0, blk, start, 0)
        pltpu.make_async_copy(x_hbm.at[pl.ds(0, blk)], buf_ref, sem).wait()
        out_ref[...] = buf_ref[...].astype(BF16)


def _moe_gather(x1, row_tok, n_valid):
    blk = MOE_BLK
    return pl.pallas_call(
        _gather_kernel,
        out_shape=jax.ShapeDtypeStruct((MOE_ROWS, D_MODEL), BF16),
        grid_spec=pltpu.PrefetchScalarGridSpec(
            num_scalar_prefetch=2,
            grid=(MOE_NBLK,),
            in_specs=[pl.BlockSpec(memory_space=pl.ANY)],
            out_specs=pl.BlockSpec((blk, D_MODEL), lambda i, tok, nv: (jnp.minimum(i, nv[0] - 1), 0)),
            scratch_shapes=[pltpu.VMEM((blk, D_MODEL), F32), pltpu.SemaphoreType.DMA],
        ),
        compiler_params=_params(1),
        name="moe_gather",
    )(row_tok, n_valid, x1)


def _moe_ffn_kernel(be_ref, nv_ref, xs_ref, w1_ref, w3_ref, w2_ref, ys_ref):
    i = pl.program_id(0)
    f = pl.program_id(1)

    @pl.when(i < nv_ref[0])
    def _():
        @pl.when(f == 0)
        def _():
            ys_ref[...] = jnp.zeros_like(ys_ref)

        w13 = jnp.concatenate([w1_ref[...].astype(BF16), w3_ref[...].astype(BF16)], axis=1)
        h = jnp.dot(xs_ref[...], w13, preferred_element_type=F32)
        h1, h3 = h[:, :MOE_TF], h[:, MOE_TF:]
        col = f * MOE_TF + lax.broadcasted_iota(jnp.int32, (1, MOE_TF), 1)
        g = jnp.where(col < D_FF, jax.nn.silu(h1) * h3, 0.0).astype(BF16)
        row = f * MOE_TF + lax.broadcasted_iota(jnp.int32, (MOE_TF, 1), 0)
        w2 = jnp.where(row < D_FF, w2_ref[...], 0.0).astype(BF16)
        ys_ref[...] += jnp.dot(g, w2, preferred_element_type=F32)


def _moe_ffn(xs, block_e, n_valid, w1, w3, w2, layer):
    blk, tf = MOE_BLK, MOE_TF

    def blk_idx(i, f, be, nv):
        return jnp.minimum(i, nv[0] - 1)

    def f_idx(i, f, be, nv):
        return jnp.where(i < nv[0], f, MOE_NF - 1)

    up = pl.BlockSpec((None, None, D_MODEL, tf),
                      lambda i, f, be, nv: (layer, be[blk_idx(i, f, be, nv)], 0, f_idx(i, f, be, nv)))
    down = pl.BlockSpec((None, None, tf, D_MODEL),
                        lambda i, f, be, nv: (layer, be[blk_idx(i, f, be, nv)], f_idx(i, f, be, nv), 0))
    rows = pl.BlockSpec((blk, D_MODEL), lambda i, f, be, nv: (blk_idx(i, f, be, nv), 0))
    return pl.pallas_call(
        _moe_ffn_kernel,
        out_shape=jax.ShapeDtypeStruct((MOE_ROWS, D_MODEL), F32),
        grid_spec=pltpu.PrefetchScalarGridSpec(
            num_scalar_prefetch=2,
            grid=(MOE_NBLK, MOE_NF),
            in_specs=[rows, up, up, down],
            out_specs=rows,
        ),
        compiler_params=_params(2),
        name="moe_ffn",
    )(block_e, n_valid, xs, w1, w3, w2)


def _combine_ln_kernel(d0_ref, d1_ref, x1_ref, gt_ref, ys_hbm, g_ref, b_ref, x2_ref, x2b_ref, buf_ref, sem):
    i = pl.program_id(0)
    tm = x1_ref.shape[0]

    def start(r, carry):
        t = i * tm + r
        pltpu.make_async_copy(ys_hbm.at[pl.ds(d0_ref[t], 1)], buf_ref.at[0, pl.ds(r, 1)], sem).start()
        pltpu.make_async_copy(ys_hbm.at[pl.ds(d1_ref[t], 1)], buf_ref.at[1, pl.ds(r, 1)], sem).start()
        return carry

    lax.fori_loop(0, tm, start, 0)
    pltpu.make_async_copy(ys_hbm.at[pl.ds(0, tm)], buf_ref.at[0], sem).wait()
    pltpu.make_async_copy(ys_hbm.at[pl.ds(0, tm)], buf_ref.at[1], sem).wait()
    gt = gt_ref[...]
    ffn = buf_ref[0] * gt[:, 0:1] + buf_ref[1] * gt[:, 1:2]
    x2 = _layer_norm(DEEPNORM_ALPHA * x1_ref[...] + ffn, g_ref[...], b_ref[...])
    x2_ref[...] = x2
    x2b_ref[...] = x2.astype(BF16)


def _combine_ln(x1, ys, dest0, dest1, gates_t, ln_g, ln_b):
    tm = COMB_TM
    row = lambda width: pl.BlockSpec((tm, width), lambda i, d0, d1: (i, 0))
    vec = pl.BlockSpec((1, D_MODEL), lambda i, d0, d1: (0, 0))
    return pl.pallas_call(
        _combine_ln_kernel,
        out_shape=(jax.ShapeDtypeStruct((N_TOK, D_MODEL), F32),
                   jax.ShapeDtypeStruct((N_TOK, D_MODEL), BF16)),
        grid_spec=pltpu.PrefetchScalarGridSpec(
            num_scalar_prefetch=2,
            grid=(N_TOK // tm,),
            in_specs=[row(D_MODEL), row(TOP_K), pl.BlockSpec(memory_space=pl.ANY), vec, vec],
            out_specs=(row(D_MODEL), row(D_MODEL)),
            scratch_shapes=[pltpu.VMEM((TOP_K, tm, D_MODEL), F32), pltpu.SemaphoreType.DMA],
        ),
        compiler_params=_params(1),
        name="combine_ln",
    )(dest0, dest1, x1, gates_t, ys, ln_g.reshape(1, D_MODEL), ln_b.reshape(1, D_MODEL))


def kernel(x, a_w_in, a_conv_w, a_w_out, kv_w, b_w_q, b_w_o, router_w, router_bias,
           moe_w1, moe_w3, moe_w2, ln1_g, ln1_b, ln2_g, ln2_b):
    x = x.reshape(N_TOK, D_MODEL)
    xb = x.astype(BF16)
    rw_pad = jnp.pad(router_w.astype(F32), ((0, 0), (0, LANES - N_EXPERTS)))
    rb_col = router_bias.astype(F32).reshape(N_EXPERTS, 1)
    k_tables = _rope_tables(1.0)
    q_tables = _rope_tables(1.0 / math.sqrt(HEAD_DIM))
    k_sh = v_sh = None
    for i in range(DEPTH):
        if i < N_A_LAYERS:
            y = _conv_proj(xb, a_w_in, a_conv_w, i)
            lhs, w_out = [y], a_w_out[i].astype(BF16)
        else:
            j = i - N_A_LAYERS
            if k_sh is None:
                k_sh = _proj(xb, kv_w[None], 0, 0, N_BRANCH, k_tables, rope=True)
                v_sh = _proj(xb, kv_w[None], 0, N_BRANCH, N_BRANCH, k_tables, rope=False)
            q = _proj(xb, b_w_q, j, 0, N_BRANCH, q_tables, rope=True)
            outs = [_attn_branch(q[g], k_sh[g], v_sh[g], BRANCHES[g][1]) for g in range(N_BRANCH)]
            lhs, w_out = [o for o, _ in outs] + [l for _, l in outs], b_w_o[j].astype(BF16)
        x1, eidx, gates = _out_ln_route(lhs, w_out, x, ln1_g[i], ln1_b[i], rw_pad, rb_col)
        row_tok, block_e, n_valid, dest0, dest1, gates_t = _build_plan(eidx, gates)
        xs = _moe_gather(x1, row_tok, n_valid)
        ys = _moe_ffn(xs, block_e, n_valid, moe_w1, moe_w3, moe_w2, i)
        x, xb = _combine_ln(x1, ys, dest0, dest1, gates_t, ln2_g[i], ln2_b[i])
    return x.reshape(BATCH, SEQ, D_MODEL)
```

```python
import functools
import math

import jax
import jax.numpy as jnp
from jax import lax
from jax.experimental import pallas as pl
from jax.experimental.pallas import tpu as pltpu

D_MODEL = 2048
BATCH = 4
SEQ = 4096
DEPTH = 4
N_TOK = BATCH * SEQ
N_A_LAYERS = DEPTH // 2
CONV_WIDTH = 3
BRANCHES = ((128, 1), (512, 4), (2048, 16))
N_BRANCH = len(BRANCHES)
HEADS_PER_BRANCH = 8
HEAD_DIM = 128
BRANCH_WIDTH = HEADS_PER_BRANCH * HEAD_DIM
SPAN = 128
ROT_DIM = HEAD_DIM // 4
ROPE_THETA = 500000.0
N_EXPERTS = 16
N_GROUPS = 4
EXPERTS_PER_GROUP = N_EXPERTS // N_GROUPS
TOP_K = 2
D_FF = 1408
DEEPNORM_ALPHA = (2.0 * DEPTH) ** 0.25
LN_EPS = 1e-5

F32 = jnp.float32
BF16 = jnp.bfloat16
U32 = jnp.uint32

VMEM_LIMIT_BYTES = 56 * 1024 * 1024
LANES = 128
HALF = D_MODEL // 2
PACK_ROWS = HALF // LANES

CONV_TM, CONV_TN = 1024, 256
PROJ_TM = 512
OUT_TM = 256
ATTN_QB = 256
MOE_BLK = 1024
MOE_TF = 256
MOE_NF = -(-D_FF // MOE_TF)
MOE_NBLK = (N_TOK * TOP_K) // MOE_BLK + N_EXPERTS
MOE_ROWS = MOE_NBLK * MOE_BLK
DISPATCH_CHUNK = 512
COMB_TM = 512
DMA_UNROLL = 8


def _params(n_axes):
    return pltpu.CompilerParams(
        dimension_semantics=("arbitrary",) * n_axes,
        vmem_limit_bytes=VMEM_LIMIT_BYTES,
    )


def _pack_pair(lo, hi):
    def rne(v):
        b = lax.bitcast_convert_type(v, U32)
        return (b + U32(0x7FFF) + ((b >> 16) & U32(1))) >> 16
    return rne(lo) | (rne(hi) << 16)


def _unpack_pair(w):
    lo = lax.bitcast_convert_type(w << 16, F32)
    hi = lax.bitcast_convert_type(w & U32(0xFFFF0000), F32)
    return lo, hi


def _store_packed(dst_ref, rows, v):
    packed = _pack_pair(v[:, :HALF], v[:, HALF:])
    for c in range(PACK_ROWS):
        dst_ref[pl.ds(c, rows, stride=PACK_ROWS), :] = packed[:, c * LANES:(c + 1) * LANES]


def _conv_proj_kernel(x_ref, wb_ref, wc_ref, wh_ref, cw_ref, y_ref, wbf_ref, ext_ref):
    i = pl.program_id(1)
    tm = x_ref.shape[0]

    @pl.when(i == 0)
    def _():
        wbf_ref[0] = wb_ref[...].astype(BF16)
        wbf_ref[1] = wc_ref[...].astype(BF16)
        wbf_ref[2] = wh_ref[...].astype(BF16)

    @pl.when(i % (SEQ // tm) == 0)
    def _():
        ext_ref[0:8, :] = jnp.zeros((8, ext_ref.shape[1]), F32)

    x = x_ref[...]
    gate_b = jnp.dot(x, wbf_ref[0], preferred_element_type=F32)
    gate_c = jnp.dot(x, wbf_ref[1], preferred_element_type=F32)
    h = jnp.dot(x, wbf_ref[2], preferred_element_type=F32)
    u = gate_c * h
    ext_ref[8:8 + tm, :] = u
    u1 = ext_ref[7:7 + tm, :]
    u2 = ext_ref[6:6 + tm, :]
    cw = cw_ref[...]
    conv = cw[2:3, :] * u + cw[1:2, :] * u1 + cw[0:1, :] * u2
    y_ref[...] = (gate_b * conv).astype(BF16)
    ext_ref[0:8, :] = ext_ref[tm:tm + 8, :]


def _conv_proj(xb, w_in, conv_w, layer):
    tm, tn = CONV_TM, CONV_TN
    nj = D_MODEL // tn
    wspec = lambda off: pl.BlockSpec((None, D_MODEL, tn), lambda j, i: (layer, 0, j + off * nj))
    return pl.pallas_call(
        _conv_proj_kernel,
        out_shape=jax.ShapeDtypeStruct((N_TOK, D_MODEL), BF16),
        grid=(nj, N_TOK // tm),
        in_specs=[
            pl.BlockSpec((tm, D_MODEL), lambda j, i: (i, 0)),
            wspec(0), wspec(1), wspec(2),
            pl.BlockSpec((None, CONV_WIDTH, tn), lambda j, i: (layer, 0, j)),
        ],
        out_specs=pl.BlockSpec((tm, tn), lambda j, i: (i, j)),
        scratch_shapes=[pltpu.VMEM((3, D_MODEL, tn), BF16), pltpu.VMEM((tm + 8, tn), F32)],
        compiler_params=_params(2),
        name="conv_proj",
    )(xb, w_in, w_in, w_in, conv_w)


def _proj_kernel(x_ref, w_ref, c_ref, s1_ref, s2_ref, o_ref, wbf_ref, rot_ref, *, rope, dilation):
    tm = x_ref.shape[0]

    @pl.when(pl.program_id(0) == 0)
    def _():
        wbf_ref[...] = w_ref[...].astype(BF16)

    acc = jnp.dot(x_ref[...], wbf_ref[...], preferred_element_type=F32)
    if rope:
        c = c_ref[...]
        s1 = s1_ref[...]
        s2 = s2_ref[...]
    for hd in range(HEADS_PER_BRANCH):
        sl = slice(hd * HEAD_DIM, (hd + 1) * HEAD_DIM)
        t = acc[:, sl]
        if rope:
            t = t * c + pltpu.roll(t, HEAD_DIM - ROT_DIM // 2, 1) * s1 + pltpu.roll(t, ROT_DIM // 2, 1) * s2
        if dilation == 1:
            o_ref[:, sl] = t.astype(BF16)
        else:
            rot_ref[hd] = t
            for r in range(dilation):
                piece = rot_ref.at[hd][pl.ds(r, tm // dilation, stride=dilation), :]
                o_ref[:, r * BRANCH_WIDTH + hd * HEAD_DIM:r * BRANCH_WIDTH + (hd + 1) * HEAD_DIM] = piece.astype(BF16)


def _proj(xb, w3d, layer, col, dilation, tables, rope):
    tm = PROJ_TM
    w = BRANCH_WIDTH
    tab = pl.BlockSpec((tm, HEAD_DIM), lambda i: (i % (SEQ // tm), 0))
    return pl.pallas_call(
        functools.partial(_proj_kernel, rope=rope, dilation=dilation),
        out_shape=jax.ShapeDtypeStruct((N_TOK // dilation, dilation * w), BF16),
        grid=(N_TOK // tm,),
        in_specs=[
            pl.BlockSpec((tm, D_MODEL), lambda i: (i, 0)),
            pl.BlockSpec((None, D_MODEL, w), lambda i: (layer, 0, col)),
            tab, tab, tab,
        ],
        out_specs=pl.BlockSpec((tm // dilation, dilation * w), lambda i: (i, 0)),
        scratch_shapes=[pltpu.VMEM((D_MODEL, w), BF16), pltpu.VMEM((HEADS_PER_BRANCH, tm, HEAD_DIM), F32)],
        compiler_params=_params(1),
        name=("proj_rope" if rope else "proj") + f"_d{dilation}",
    )(xb, w3d, *tables)


def _rope_tables(scale):
    inv_freq = ROPE_THETA ** (-jnp.arange(0, ROT_DIM, 2, dtype=F32) / ROT_DIM)
    ang = jnp.arange(SEQ, dtype=F32)[:, None] * inv_freq[None, :]
    cos, sin = jnp.cos(ang), jnp.sin(ang)
    half = ROT_DIM // 2
    rest = HEAD_DIM - ROT_DIM
    c = jnp.concatenate([cos, cos, jnp.ones((SEQ, rest), F32)], axis=1) * scale
    s1 = jnp.concatenate([-sin, jnp.zeros((SEQ, HEAD_DIM - half), F32)], axis=1) * scale
    s2 = jnp.concatenate([jnp.zeros((SEQ, half), F32), sin, jnp.zeros((SEQ, rest), F32)], axis=1) * scale
    return c, s1, s2


def _attn_kernel(q_ref, kc_ref, kp_ref, vc_ref, vp_ref, o_ref, l_ref):
    qb = q_ref.shape[0]
    qi = lax.broadcasted_iota(jnp.int32, (SPAN, 2 * SPAN), 0)
    kj = lax.broadcasted_iota(jnp.int32, (SPAN, 2 * SPAN), 1)
    dist = SPAN + qi - kj
    band = (dist >= 0) & (dist <= SPAN)
    first_key = jnp.where(pl.program_id(1) == 0, SPAN, 0)
    band_first = band & (kj >= first_key)
    for a in range(qb // SPAN):
        rows = slice(a * SPAN, (a + 1) * SPAN)
        for hd in range(HEADS_PER_BRANCH):
            cols = slice(hd * HEAD_DIM, (hd + 1) * HEAD_DIM)
            q = q_ref[rows, cols]
            if a == 0:
                k_prev, v_prev = kp_ref[:, cols], vp_ref[:, cols]
                mask = band_first
            else:
                prows = slice((a - 1) * SPAN, a * SPAN)
                k_prev, v_prev = kc_ref[prows, cols], vc_ref[prows, cols]
                mask = band
            kk = jnp.concatenate([k_prev, kc_ref[rows, cols]], axis=0)
            vv = jnp.concatenate([v_prev, vc_ref[rows, cols]], axis=0)
            s = lax.dot_general(q, kk, (((1,), (1,)), ((), ())), preferred_element_type=F32)
            s = jnp.where(mask, s, -jnp.inf)
            m = jnp.max(s, axis=-1, keepdims=True)
            p = jnp.exp(s - m)
            den = jnp.sum(p, axis=-1, keepdims=True)
            o = jnp.dot(p.astype(BF16), vv, preferred_element_type=F32) / den
            o_ref[rows, cols] = o.astype(BF16)
            l_ref[rows, cols] = jnp.broadcast_to(m + jnp.log(den), (SPAN, HEAD_DIM))


def _attn_branch(q, k, v, dilation):
    sub = SEQ // dilation
    qb = min(ATTN_QB, sub)
    w = BRANCH_WIDTH
    view = lambda t: t.reshape(BATCH, sub, dilation * w)
    cur = pl.BlockSpec((None, qb, w), lambda b, n, r: (b, n, r))
    prev = pl.BlockSpec((None, SPAN, w),
                        lambda b, n, r: (b, jnp.maximum(n * (qb // SPAN) - 1, 0), r))
    o, l = pl.pallas_call(
        _attn_kernel,
        out_shape=(jax.ShapeDtypeStruct((BATCH, sub, dilation * w), BF16),
                   jax.ShapeDtypeStruct((BATCH, sub, dilation * w), F32)),
        grid=(BATCH, sub // qb, dilation),
        in_specs=[cur, cur, prev, cur, prev],
        out_specs=(cur, cur),
        compiler_params=_params(3),
        name=f"dilated_attn_d{dilation}",
    )(view(q), view(k), view(k), view(v), view(v))
    return o.reshape(N_TOK // dilation, dilation * w), l.reshape(N_TOK // dilation, dilation * w)


def _layer_norm(z, g, b):
    mean = jnp.mean(z, axis=-1, keepdims=True)
    zc = z - mean
    var = jnp.mean(zc * zc, axis=-1, keepdims=True)
    return zc * lax.rsqrt(var + LN_EPS) * g + b


def _route(logits_t, bias_col):
    scores = jax.nn.sigmoid(logits_t)
    sel = scores + bias_col
    sel_r = [sel[e:e + 1, :] for e in range(N_EXPERTS)]
    sc_r = [scores[e:e + 1, :] for e in range(N_EXPERTS)]
    best_g = None
    for g in range(N_GROUPS):
        v = sel_r[g * EXPERTS_PER_GROUP:(g + 1) * EXPERTS_PER_GROUP]
        gs = None
        for a in range(EXPERTS_PER_GROUP):
            for b in range(a + 1, EXPERTS_PER_GROUP):
                ps = v[a] + v[b]
                gs = ps if gs is None else jnp.maximum(gs, ps)
        if best_g is None:
            best_g, gidx = gs, jnp.zeros_like(gs, dtype=jnp.int32)
        else:
            upd = gs > best_g
            best_g = jnp.where(upd, gs, best_g)
            gidx = jnp.where(upd, g, gidx)
    cand, raw = [], []
    for j in range(EXPERTS_PER_GROUP):
        cv, rv = sel_r[j], sc_r[j]
        for g in range(1, N_GROUPS):
            pick = gidx == g
            cv = jnp.where(pick, sel_r[g * EXPERTS_PER_GROUP + j], cv)
            rv = jnp.where(pick, sc_r[g * EXPERTS_PER_GROUP + j], rv)
        cand.append(cv)
        raw.append(rv)

    def argmax4(vals):
        best, idx = vals[0], jnp.zeros_like(gidx)
        for j in range(1, EXPERTS_PER_GROUP):
            upd = vals[j] > best
            best = jnp.where(upd, vals[j], best)
            idx = jnp.where(upd, j, idx)
        return idx

    i1 = argmax4(cand)
    i2 = argmax4([jnp.where(i1 == j, -jnp.inf, cand[j]) for j in range(EXPERTS_PER_GROUP)])
    pick_raw = lambda idx: sum(jnp.where(idx == j, raw[j], 0.0) for j in range(EXPERTS_PER_GROUP))
    g1, g2 = pick_raw(i1), pick_raw(i2)
    tot = g1 + g2
    e1 = gidx * EXPERTS_PER_GROUP + i1
    e2 = gidx * EXPERTS_PER_GROUP + i2
    return e1, e2, g1 / tot, g2 / tot


def _out_ln_route_tail(mix, x_ref, g_ref, b_ref, rw_ref, rb_ref, x1_ref, xp_ref, e_ref, gt_ref):
    tm = x_ref.shape[0]
    z = DEEPNORM_ALPHA * x_ref[...] + mix
    x1 = _layer_norm(z, g_ref[...], b_ref[...])
    x1_ref[...] = x1
    _store_packed(xp_ref, tm, x1)
    x_hi = x1.astype(BF16)
    x_lo = (x1 - x_hi.astype(F32)).astype(BF16)
    p_hi = jnp.dot(x_hi, rw_ref[...], preferred_element_type=F32)
    p_lo = jnp.dot(x_lo, rw_ref[:, :LANES], preferred_element_type=F32)
    logits = p_hi[:, :LANES] + (p_hi[:, LANES:] + p_lo)
    logits_t = logits.T[0:N_EXPERTS, :]
    e1, e2, g1, g2 = _route(logits_t, rb_ref[...])
    e_ref[0:1, :] = e1
    e_ref[1:2, :] = e2
    gt_ref[0:1, :] = g1
    gt_ref[1:2, :] = g2


def _out_ln_route_a_kernel(a_ref, w_ref, *rest):
    mix = jnp.dot(a_ref[...], w_ref[...], preferred_element_type=F32)
    _out_ln_route_tail(mix, *rest)


def _out_ln_route_b_kernel(o0_ref, o1_ref, o2_ref, l0_ref, l1_ref, l2_ref, w_ref,
                           x_ref, g_ref, b_ref, rw_ref, rb_ref, x1_ref, xp_ref, e_ref, gt_ref,
                           os_ref, ls_ref):
    tm = x_ref.shape[0]
    merged = []
    for hd in range(HEADS_PER_BRANCH):
        for g, (o_ref, l_ref) in enumerate(((o1_ref, l1_ref), (o2_ref, l2_ref))):
            d = BRANCHES[g + 1][1]
            for r in range(d):
                cols = slice(r * BRANCH_WIDTH + hd * HEAD_DIM, r * BRANCH_WIDTH + (hd + 1) * HEAD_DIM)
                os_ref.at[g, hd][pl.ds(r, tm // d, stride=d), :] = o_ref[:, cols].astype(F32)
                ls_ref.at[g, hd][pl.ds(r, tm // d, stride=d), :] = l_ref[:, cols]
        cols = slice(hd * HEAD_DIM, (hd + 1) * HEAD_DIM)
        l0, l1, l2 = l0_ref[:, cols], ls_ref[0, hd], ls_ref[1, hd]
        lm = jnp.maximum(jnp.maximum(l0, l1), l2)
        w0, w1, w2 = jnp.exp(l0 - lm), jnp.exp(l1 - lm), jnp.exp(l2 - lm)
        m = (o0_ref[:, cols].astype(F32) * w0 + os_ref[0, hd] * w1 + os_ref[1, hd] * w2) / (w0 + w1 + w2)
        merged.append(m.astype(BF16))
    mix = jnp.dot(jnp.concatenate(merged, axis=1), w_ref[...], preferred_element_type=F32)
    _out_ln_route_tail(mix, x_ref, g_ref, b_ref, rw_ref, rb_ref, x1_ref, xp_ref, e_ref, gt_ref)


def _out_ln_route(lhs, w_bf, x, ln_g, ln_b, rw_split, rb_col):
    tm = OUT_TM
    kdim = w_bf.shape[0]
    row = lambda width: pl.BlockSpec((tm, width), lambda i: (i, 0))
    full = lambda shape: pl.BlockSpec(shape, lambda i: (0,) * len(shape))
    lane_row = pl.BlockSpec((TOP_K, tm), lambda i: (0, i))
    if len(lhs) == 1:
        body, name, lhs_specs, scratch = _out_ln_route_a_kernel, "out_ln_route_conv", [row(kdim)], []
    else:
        body, name = _out_ln_route_b_kernel, "out_ln_route_attn"
        blocked = [pl.BlockSpec((tm // d, d * BRANCH_WIDTH), lambda i: (i, 0)) for _, d in BRANCHES]
        lhs_specs = blocked + blocked
        scratch = [pltpu.VMEM((N_BRANCH - 1, HEADS_PER_BRANCH, tm, HEAD_DIM), F32)] * 2
    return pl.pallas_call(
        body,
        out_shape=(jax.ShapeDtypeStruct((N_TOK, D_MODEL), F32),
                   jax.ShapeDtypeStruct((N_TOK * PACK_ROWS, LANES), U32),
                   jax.ShapeDtypeStruct((TOP_K, N_TOK), jnp.int32),
                   jax.ShapeDtypeStruct((TOP_K, N_TOK), F32)),
        grid=(N_TOK // tm,),
        in_specs=lhs_specs + [
            full((kdim, D_MODEL)), row(D_MODEL), full((1, D_MODEL)), full((1, D_MODEL)),
            full((D_MODEL, 2 * LANES)), full((N_EXPERTS, 1)),
        ],
        out_specs=(row(D_MODEL), pl.BlockSpec((tm * PACK_ROWS, LANES), lambda i: (i, 0)), lane_row, lane_row),
        scratch_shapes=scratch,
        compiler_params=_params(1),
        name=name,
    )(*lhs, w_bf, x, ln_g.reshape(1, D_MODEL), ln_b.reshape(1, D_MODEL), rw_split, rb_col)


def _build_plan(eidx, gates):
    flat_e = eidx.reshape(-1)
    onehot = (flat_e[:, None] == jnp.arange(N_EXPERTS, dtype=jnp.int32)[None, :]).astype(jnp.int32)
    csum = jnp.cumsum(onehot, axis=0)
    rank = jnp.sum(onehot * csum, axis=1) - 1
    counts = csum[-1]
    nblk_e = ((counts + MOE_BLK - 1) // MOE_BLK).astype(jnp.int32)
    blk_end = jnp.cumsum(nblk_e).astype(jnp.int32)
    blk_start = blk_end - nblk_e
    dest = (blk_start[flat_e] * MOE_BLK + rank).astype(jnp.int32)
    n_valid = blk_end[-1:]
    block_e = jnp.clip(jnp.searchsorted(blk_end, jnp.arange(MOE_NBLK, dtype=jnp.int32), side="right"),
                       0, N_EXPERTS - 1).astype(jnp.int32)
    return dest[:N_TOK], dest[N_TOK:], blk_end - 1, nblk_e, block_e, n_valid, gates.T


def _dispatch_kernel(d0_ref, d1_ref, last_ref, nblk_ref, xp_hbm, xs_hbm, zero_ref, zsem, sems):
    blk_rows = MOE_BLK * PACK_ROWS
    chunk = DISPATCH_CHUNK

    zero_ref[...] = jnp.zeros_like(zero_ref)

    def zero_copy(e):
        start = pl.multiple_of(last_ref[e] * blk_rows, blk_rows)
        return pltpu.make_async_copy(zero_ref, xs_hbm.at[pl.ds(start, blk_rows)], zsem)

    for e in range(N_EXPERTS):
        @pl.when(nblk_ref[e] > 0)
        def _():
            zero_copy(e).start()
    for e in range(N_EXPERTS):
        @pl.when(nblk_ref[e] > 0)
        def _():
            zero_copy(e).wait()

    def issue(k):
        sem = sems.at[k % 2]

        def body(r, carry):
            t = k * chunk + r
            src = xp_hbm.at[pl.ds(pl.multiple_of(t * PACK_ROWS, PACK_ROWS), PACK_ROWS)]
            for d_ref in (d0_ref, d1_ref):
                dst = xs_hbm.at[pl.ds(pl.multiple_of(d_ref[t] * PACK_ROWS, PACK_ROWS), PACK_ROWS)]
                pltpu.make_async_copy(src, dst, sem).start()
            return carry

        lax.fori_loop(0, chunk, body, 0, unroll=DMA_UNROLL)

    def drain(k):
        whole = pl.ds(0, chunk * PACK_ROWS)
        for _ in range(TOP_K):
            pltpu.make_async_copy(xp_hbm.at[whole], xs_hbm.at[whole], sems.at[k % 2]).wait()

    n_chunks = N_TOK // chunk

    def step(k, carry):
        issue(k)

        @pl.when(k > 0)
        def _():
            drain(k - 1)

        return carry

    lax.fori_loop(0, n_chunks, step, 0)
    drain(n_chunks - 1)


def _moe_dispatch(xp, dest0, dest1, last_blk, nblk_e):
    return pl.pallas_call(
        _dispatch_kernel,
        out_shape=jax.ShapeDtypeStruct((MOE_ROWS * PACK_ROWS, LANES), U32),
        grid_spec=pltpu.PrefetchScalarGridSpec(
            num_scalar_prefetch=4,
            grid=(1,),
            in_specs=[pl.BlockSpec(memory_space=pl.ANY)],
            out_specs=pl.BlockSpec(memory_space=pl.ANY),
            scratch_shapes=[pltpu.VMEM((MOE_BLK * PACK_ROWS, LANES), U32),
                            pltpu.SemaphoreType.DMA, pltpu.SemaphoreType.DMA((2,))],
        ),
        compiler_params=_params(1),
        name="moe_dispatch",
    )(dest0, dest1, last_blk, nblk_e, xp)


def _moe_ffn_kernel(be_ref, nv_ref, xs_ref, w1_ref, w3_ref, w2_ref, ys_ref, xb_ref, acc_ref):
    i = pl.program_id(0)
    f = pl.program_id(1)
    blk = MOE_BLK

    @pl.when(i < nv_ref[0])
    def _():
        @pl.when(f == 0)
        def _():
            for c in range(PACK_ROWS):
                lo, hi = _unpack_pair(xs_ref[pl.ds(c, blk, stride=PACK_ROWS), :])
                xb_ref[:, c * LANES:(c + 1) * LANES] = lo.astype(BF16)
                xb_ref[:, HALF + c * LANES:HALF + (c + 1) * LANES] = hi.astype(BF16)
            acc_ref[...] = jnp.zeros_like(acc_ref)

        w13 = jnp.concatenate([w1_ref[...].astype(BF16), w3_ref[...].astype(BF16)], axis=1)
        h = jnp.dot(xb_ref[...], w13, preferred_element_type=F32)
        h1, h3 = h[:, :MOE_TF], h[:, MOE_TF:]
        col = f * MOE_TF + lax.broadcasted_iota(jnp.int32, (1, MOE_TF), 1)
        g = jnp.where(col < D_FF, jax.nn.silu(h1) * h3, 0.0).astype(BF16)
        row = f * MOE_TF + lax.broadcasted_iota(jnp.int32, (MOE_TF, 1), 0)
        w2 = jnp.where(row < D_FF, w2_ref[...], 0.0).astype(BF16)
        acc_ref[...] += jnp.dot(g, w2, preferred_element_type=F32)

        @pl.when(f == MOE_NF - 1)
        def _():
            _store_packed(ys_ref, blk, acc_ref[...])


def _moe_ffn(xs, block_e, n_valid, w1, w3, w2, layer):
    blk, tf = MOE_BLK, MOE_TF

    def blk_idx(i, f, be, nv):
        return jnp.minimum(i, nv[0] - 1)

    def f_idx(i, f, be, nv):
        return jnp.where(i < nv[0], f, MOE_NF - 1)

    up = pl.BlockSpec((None, None, D_MODEL, tf),
                      lambda i, f, be, nv: (layer, be[blk_idx(i, f, be, nv)], 0, f_idx(i, f, be, nv)))
    down = pl.BlockSpec((None, None, tf, D_MODEL),
                        lambda i, f, be, nv: (layer, be[blk_idx(i, f, be, nv)], f_idx(i, f, be, nv), 0))
    rows = pl.BlockSpec((blk * PACK_ROWS, LANES), lambda i, f, be, nv: (blk_idx(i, f, be, nv), 0))
    return pl.pallas_call(
        _moe_ffn_kernel,
        out_shape=jax.ShapeDtypeStruct((MOE_ROWS * PACK_ROWS, LANES), U32),
        grid_spec=pltpu.PrefetchScalarGridSpec(
            num_scalar_prefetch=2,
            grid=(MOE_NBLK, MOE_NF),
            in_specs=[rows, up, up, down],
            out_specs=rows,
            scratch_shapes=[pltpu.VMEM((blk, D_MODEL), BF16), pltpu.VMEM((blk, D_MODEL), F32)],
        ),
        compiler_params=_params(2),
        name="moe_ffn",
    )(block_e, n_valid, xs, w1, w3, w2)


def _combine_ln_kernel(d0_ref, d1_ref, x1_ref, gt_ref, ys_hbm, g_ref, b_ref, x2_ref, x2b_ref,
                       buf_ref, ffn_ref, sems):
    i = pl.program_id(0)
    tm = x1_ref.shape[0]

    def issue(tile):
        slot = tile % 2

        def body(r, carry):
            t = tile * tm + r
            for k, d_ref in enumerate((d0_ref, d1_ref)):
                src = ys_hbm.at[pl.ds(pl.multiple_of(d_ref[t] * PACK_ROWS, PACK_ROWS), PACK_ROWS)]
                dst = buf_ref.at[slot, k, pl.ds(pl.multiple_of(r * PACK_ROWS, PACK_ROWS), PACK_ROWS)]
                pltpu.make_async_copy(src, dst, sems.at[slot]).start()
            return carry

        lax.fori_loop(0, tm, body, 0, unroll=DMA_UNROLL)

    @pl.when(i == 0)
    def _():
        issue(0)

    @pl.when(i + 1 < pl.num_programs(0))
    def _():
        issue(i + 1)

    slot = i % 2
    for k in range(TOP_K):
        pltpu.make_async_copy(ys_hbm.at[pl.ds(0, tm * PACK_ROWS)], buf_ref.at[slot, k], sems.at[slot]).wait()
    gt = gt_ref[...]
    g0, g1 = gt[:, 0:1], gt[:, 1:2]
    for c in range(PACK_ROWS):
        lo0, hi0 = _unpack_pair(buf_ref.at[slot, 0][pl.ds(c, tm, stride=PACK_ROWS), :])
        lo1, hi1 = _unpack_pair(buf_ref.at[slot, 1][pl.ds(c, tm, stride=PACK_ROWS), :])
        ffn_ref[:, c * LANES:(c + 1) * LANES] = lo0 * g0 + lo1 * g1
        ffn_ref[:, HALF + c * LANES:HALF + (c + 1) * LANES] = hi0 * g0 + hi1 * g1
    x2 = _layer_norm(DEEPNORM_ALPHA * x1_ref[...] + ffn_ref[...], g_ref[...], b_ref[...])
    x2_ref[...] = x2
    x2b_ref[...] = x2.astype(BF16)


def _combine_ln(x1, ys, dest0, dest1, gates_t, ln_g, ln_b):
    tm = COMB_TM
    row = lambda width: pl.BlockSpec((tm, width), lambda i, d0, d1: (i, 0))
    vec = pl.BlockSpec((1, D_MODEL), lambda i, d0, d1: (0, 0))
    return pl.pallas_call(
        _combine_ln_kernel,
        out_shape=(jax.ShapeDtypeStruct((N_TOK, D_MODEL), F32),
                   jax.ShapeDtypeStruct((N_TOK, D_MODEL), BF16)),
        grid_spec=pltpu.PrefetchScalarGridSpec(
            num_scalar_prefetch=2,
            grid=(N_TOK // tm,),
            in_specs=[row(D_MODEL), row(TOP_K), pl.BlockSpec(memory_space=pl.ANY), vec, vec],
            out_specs=(row(D_MODEL), row(D_MODEL)),
            scratch_shapes=[pltpu.VMEM((2, TOP_K, tm * PACK_ROWS, LANES), U32),
                            pltpu.VMEM((tm, D_MODEL), F32),
                            pltpu.SemaphoreType.DMA((2,))],
        ),
        compiler_params=_params(1),
        name="combine_ln",
    )(dest0, dest1, x1, gates_t, ys, ln_g.reshape(1, D_MODEL), ln_b.reshape(1, D_MODEL))


def kernel(x, a_w_in, a_conv_w, a_w_out, kv_w, b_w_q, b_w_o, router_w, router_bias,
           moe_w1, moe_w3, moe_w2, ln1_g, ln1_b, ln2_g, ln2_b):
    x = x.reshape(N_TOK, D_MODEL)
    xb = x.astype(BF16)
    rw_pad = jnp.pad(router_w.astype(F32), ((0, 0), (0, LANES - N_EXPERTS)))
    rw_hi = rw_pad.astype(BF16)
    rw_lo = (rw_pad - rw_hi.astype(F32)).astype(BF16)
    rw_split = jnp.concatenate([rw_hi, rw_lo], axis=1)
    rb_col = router_bias.astype(F32).reshape(N_EXPERTS, 1)
    k_tables = _rope_tables(1.0)
    q_tables = _rope_tables(1.0 / math.sqrt(HEAD_DIM))
    dils = [d for _, d in BRANCHES]
    k_sh = v_sh = None
    for i in range(DEPTH):
        if i < N_A_LAYERS:
            y = _conv_proj(xb, a_w_in, a_conv_w, i)
            lhs, w_out = [y], a_w_out[i].astype(BF16)
        else:
            j = i - N_A_LAYERS
            if k_sh is None:
                kv3 = kv_w[None]
                k_sh = [_proj(xb, kv3, 0, g, dils[g], k_tables, rope=True) for g in range(N_BRANCH)]
                v_sh = [_proj(xb, kv3, 0, N_BRANCH + g, dils[g], k_tables, rope=False) for g in range(N_BRANCH)]
            q = [_proj(xb, b_w_q, j, g, dils[g], q_tables, rope=True) for g in range(N_BRANCH)]
            outs = [_attn_branch(q[g], k_sh[g], v_sh[g], dils[g]) for g in range(N_BRANCH)]
            lhs, w_out = [o for o, _ in outs] + [l for _, l in outs], b_w_o[j].astype(BF16)
        x1, xp, eidx, gates = _out_ln_route(lhs, w_out, x, ln1_g[i], ln1_b[i], rw_split, rb_col)
        dest0, dest1, last_blk, nblk_e, block_e, n_valid, gates_t = _build_plan(eidx, gates)
        xs = _moe_dispatch(xp, dest0, dest1, last_blk, nblk_e)
        ys = _moe_ffn(xs, block_e, n_valid, moe_w1, moe_w3, moe_w2, i)
        x, xb = _combine_ln(x1, ys, dest0, dest1, gates_t, ln2_g[i], ln2_b[i])
    return x.reshape(BATCH, SEQ, D_MODEL)
```

```python
import functools
import math

import jax
import jax.numpy as jnp
from jax import lax
from jax.experimental import pallas as pl
from jax.experimental.pallas import tpu as pltpu

D_MODEL = 2048
BATCH = 4
SEQ = 4096
DEPTH = 4
N_TOK = BATCH * SEQ
N_A_LAYERS = DEPTH // 2
CONV_WIDTH = 3
BRANCHES = ((128, 1), (512, 4), (2048, 16))
N_BRANCH = len(BRANCHES)
HEADS_PER_BRANCH = 8
HEAD_DIM = 128
BRANCH_WIDTH = HEADS_PER_BRANCH * HEAD_DIM
SPAN = 128
ROT_DIM = HEAD_DIM // 4
ROPE_THETA = 500000.0
N_EXPERTS = 16
N_GROUPS = 4
EXPERTS_PER_GROUP = N_EXPERTS // N_GROUPS
TOP_K = 2
D_FF = 1408
DEEPNORM_ALPHA = (2.0 * DEPTH) ** 0.25
LN_EPS = 1e-5

F32 = jnp.float32
BF16 = jnp.bfloat16
U32 = jnp.uint32

VMEM_LIMIT_BYTES = 56 * 1024 * 1024
LANES = 128
HALF = D_MODEL // 2
PACK_ROWS = HALF // LANES

CONV_TM, CONV_TN = 1024, 256
PROJ_TM = 512
OUT_TM = 256
ATTN_QB = 256
MOE_BLK = 1024
MOE_TF = 256
MOE_NF = -(-D_FF // MOE_TF)
MOE_NBLK = (N_TOK * TOP_K) // MOE_BLK + N_EXPERTS
MOE_ROWS = MOE_NBLK * MOE_BLK
MOE_SUB = 256
DISPATCH_TM = 512
COMB_TM = 512
DMA_UNROLL = 8


def _params(n_axes):
    return pltpu.CompilerParams(
        dimension_semantics=("arbitrary",) * n_axes,
        vmem_limit_bytes=VMEM_LIMIT_BYTES,
    )


def _pack_pair(lo, hi):
    def rne(v):
        b = lax.bitcast_convert_type(v, U32)
        return (b + U32(0x7FFF) + ((b >> 16) & U32(1))) >> 16
    return rne(lo) | (rne(hi) << 16)


def _unpack_pair(w):
    lo = lax.bitcast_convert_type(w << 16, F32)
    hi = lax.bitcast_convert_type(w & U32(0xFFFF0000), F32)
    return lo, hi


def _store_packed(dst_ref, rows, v):
    packed = _pack_pair(v[:, :HALF], v[:, HALF:])
    for c in range(PACK_ROWS):
        dst_ref[pl.ds(c, rows, stride=PACK_ROWS), :] = packed[:, c * LANES:(c + 1) * LANES]


def _conv_proj_kernel(x_ref, wb_ref, wc_ref, wh_ref, cw_ref, y_ref, wbf_ref, ext_ref):
    i = pl.program_id(1)
    tm = x_ref.shape[0]

    @pl.when(i == 0)
    def _():
        wbf_ref[0] = wb_ref[...].astype(BF16)
        wbf_ref[1] = wc_ref[...].astype(BF16)
        wbf_ref[2] = wh_ref[...].astype(BF16)

    @pl.when(i % (SEQ // tm) == 0)
    def _():
        ext_ref[0:8, :] = jnp.zeros((8, ext_ref.shape[1]), F32)

    x = x_ref[...]
    gate_b = jnp.dot(x, wbf_ref[0], preferred_element_type=F32)
    gate_c = jnp.dot(x, wbf_ref[1], preferred_element_type=F32)
    h = jnp.dot(x, wbf_ref[2], preferred_element_type=F32)
    u = gate_c * h
    ext_ref[8:8 + tm, :] = u
    u1 = ext_ref[7:7 + tm, :]
    u2 = ext_ref[6:6 + tm, :]
    cw = cw_ref[...]
    conv = cw[2:3, :] * u + cw[1:2, :] * u1 + cw[0:1, :] * u2
    y_ref[...] = (gate_b * conv).astype(BF16)
    ext_ref[0:8, :] = ext_ref[tm:tm + 8, :]


def _conv_proj(xb, w_in, conv_w, layer):
    tm, tn = CONV_TM, CONV_TN
    nj = D_MODEL // tn
    wspec = lambda off: pl.BlockSpec((None, D_MODEL, tn), lambda j, i: (layer, 0, j + off * nj))
    return pl.pallas_call(
        _conv_proj_kernel,
        out_shape=jax.ShapeDtypeStruct((N_TOK, D_MODEL), BF16),
        grid=(nj, N_TOK // tm),
        in_specs=[
            pl.BlockSpec((tm, D_MODEL), lambda j, i: (i, 0)),
            wspec(0), wspec(1), wspec(2),
            pl.BlockSpec((None, CONV_WIDTH, tn), lambda j, i: (layer, 0, j)),
        ],
        out_specs=pl.BlockSpec((tm, tn), lambda j, i: (i, j)),
        scratch_shapes=[pltpu.VMEM((3, D_MODEL, tn), BF16), pltpu.VMEM((tm + 8, tn), F32)],
        compiler_params=_params(2),
        name="conv_proj",
    )(xb, w_in, w_in, w_in, conv_w)


def _proj_kernel(x_ref, w_ref, c_ref, s1_ref, s2_ref, o_ref, wbf_ref, rot_ref, *, rope, dilation):
    tm = x_ref.shape[0]

    @pl.when(pl.program_id(0) == 0)
    def _():
        wbf_ref[...] = w_ref[...].astype(BF16)

    acc = jnp.dot(x_ref[...], wbf_ref[...], preferred_element_type=F32)
    if rope:
        c = c_ref[...]
        s1 = s1_ref[...]
        s2 = s2_ref[...]
    for hd in range(HEADS_PER_BRANCH):
        sl = slice(hd * HEAD_DIM, (hd + 1) * HEAD_DIM)
        t = acc[:, sl]
        if rope:
            t = t * c + pltpu.roll(t, HEAD_DIM - ROT_DIM // 2, 1) * s1 + pltpu.roll(t, ROT_DIM // 2, 1) * s2
        if dilation == 1:
            o_ref[:, sl] = t.astype(BF16)
        else:
            rot_ref[hd] = t
            for r in range(dilation):
                piece = rot_ref.at[hd][pl.ds(r, tm // dilation, stride=dilation), :]
                o_ref[:, r * BRANCH_WIDTH + hd * HEAD_DIM:r * BRANCH_WIDTH + (hd + 1) * HEAD_DIM] = piece.astype(BF16)


def _proj(xb, w3d, layer, col, dilation, tables, rope):
    tm = PROJ_TM
    w = BRANCH_WIDTH
    tab = pl.BlockSpec((tm, HEAD_DIM), lambda i: (i % (SEQ // tm), 0))
    return pl.pallas_call(
        functools.partial(_proj_kernel, rope=rope, dilation=dilation),
        out_shape=jax.ShapeDtypeStruct((N_TOK // dilation, dilation * w), BF16),
        grid=(N_TOK // tm,),
        in_specs=[
            pl.BlockSpec((tm, D_MODEL), lambda i: (i, 0)),
            pl.BlockSpec((None, D_MODEL, w), lambda i: (layer, 0, col)),
            tab, tab, tab,
        ],
        out_specs=pl.BlockSpec((tm // dilation, dilation * w), lambda i: (i, 0)),
        scratch_shapes=[pltpu.VMEM((D_MODEL, w), BF16), pltpu.VMEM((HEADS_PER_BRANCH, tm, HEAD_DIM), F32)],
        compiler_params=_params(1),
        name=("proj_rope" if rope else "proj") + f"_d{dilation}",
    )(xb, w3d, *tables)


def _rope_tables(scale):
    inv_freq = ROPE_THETA ** (-jnp.arange(0, ROT_DIM, 2, dtype=F32) / ROT_DIM)
    ang = jnp.arange(SEQ, dtype=F32)[:, None] * inv_freq[None, :]
    cos, sin = jnp.cos(ang), jnp.sin(ang)
    half = ROT_DIM // 2
    rest = HEAD_DIM - ROT_DIM
    c = jnp.concatenate([cos, cos, jnp.ones((SEQ, rest), F32)], axis=1) * scale
    s1 = jnp.concatenate([-sin, jnp.zeros((SEQ, HEAD_DIM - half), F32)], axis=1) * scale
    s2 = jnp.concatenate([jnp.zeros((SEQ, half), F32), sin, jnp.zeros((SEQ, rest), F32)], axis=1) * scale
    return c, s1, s2


def _attn_kernel(q_ref, kc_ref, kp_ref, vc_ref, vp_ref, o_ref, l_ref):
    qb = q_ref.shape[0]
    qi = lax.broadcasted_iota(jnp.int32, (SPAN, 2 * SPAN), 0)
    kj = lax.broadcasted_iota(jnp.int32, (SPAN, 2 * SPAN), 1)
    dist = SPAN + qi - kj
    band = (dist >= 0) & (dist <= SPAN)
    first_key = jnp.where(pl.program_id(1) == 0, SPAN, 0)
    band_first = band & (kj >= first_key)
    for a in range(qb // SPAN):
        rows = slice(a * SPAN, (a + 1) * SPAN)
        for hd in range(HEADS_PER_BRANCH):
            cols = slice(hd * HEAD_DIM, (hd + 1) * HEAD_DIM)
            q = q_ref[rows, cols]
            if a == 0:
                k_prev, v_prev = kp_ref[:, cols], vp_ref[:, cols]
                mask = band_first
            else:
                prows = slice((a - 1) * SPAN, a * SPAN)
                k_prev, v_prev = kc_ref[prows, cols], vc_ref[prows, cols]
                mask = band
            kk = jnp.concatenate([k_prev, kc_ref[rows, cols]], axis=0)
            vv = jnp.concatenate([v_prev, vc_ref[rows, cols]], axis=0)
            s = lax.dot_general(q, kk, (((1,), (1,)), ((), ())), preferred_element_type=F32)
            s = jnp.where(mask, s, -jnp.inf)
            m = jnp.max(s, axis=-1, keepdims=True)
            p = jnp.exp(s - m)
            den = jnp.sum(p, axis=-1, keepdims=True)
            o = jnp.dot(p.astype(BF16), vv, preferred_element_type=F32) / den
            o_ref[rows, cols] = o.astype(BF16)
            l_ref[rows, cols] = jnp.broadcast_to(m + jnp.log(den), (SPAN, HEAD_DIM))


def _attn_branch(q, k, v, dilation):
    sub = SEQ // dilation
    qb = min(ATTN_QB, sub)
    w = BRANCH_WIDTH
    view = lambda t: t.reshape(BATCH, sub, dilation * w)
    cur = pl.BlockSpec((None, qb, w), lambda b, n, r: (b, n, r))
    prev = pl.BlockSpec((None, SPAN, w),
                        lambda b, n, r: (b, jnp.maximum(n * (qb // SPAN) - 1, 0), r))
    o, l = pl.pallas_call(
        _attn_kernel,
        out_shape=(jax.ShapeDtypeStruct((BATCH, sub, dilation * w), BF16),
                   jax.ShapeDtypeStruct((BATCH, sub, dilation * w), F32)),
        grid=(BATCH, sub // qb, dilation),
        in_specs=[cur, cur, prev, cur, prev],
        out_specs=(cur, cur),
        compiler_params=_params(3),
        name=f"dilated_attn_d{dilation}",
    )(view(q), view(k), view(k), view(v), view(v))
    return o.reshape(N_TOK // dilation, dilation * w), l.reshape(N_TOK // dilation, dilation * w)


def _layer_norm(z, g, b):
    mean = jnp.mean(z, axis=-1, keepdims=True)
    zc = z - mean
    var = jnp.mean(zc * zc, axis=-1, keepdims=True)
    return zc * lax.rsqrt(var + LN_EPS) * g + b


def _route(logits_t, bias_col):
    scores = jax.nn.sigmoid(logits_t)
    sel = scores + bias_col
    sel_r = [sel[e:e + 1, :] for e in range(N_EXPERTS)]
    sc_r = [scores[e:e + 1, :] for e in range(N_EXPERTS)]
    best_g = None
    for g in range(N_GROUPS):
        v = sel_r[g * EXPERTS_PER_GROUP:(g + 1) * EXPERTS_PER_GROUP]
        gs = None
        for a in range(EXPERTS_PER_GROUP):
            for b in range(a + 1, EXPERTS_PER_GROUP):
                ps = v[a] + v[b]
                gs = ps if gs is None else jnp.maximum(gs, ps)
        if best_g is None:
            best_g, gidx = gs, jnp.zeros_like(gs, dtype=jnp.int32)
        else:
            upd = gs > best_g
            best_g = jnp.where(upd, gs, best_g)
            gidx = jnp.where(upd, g, gidx)
    cand, raw = [], []
    for j in range(EXPERTS_PER_GROUP):
        cv, rv = sel_r[j], sc_r[j]
        for g in range(1, N_GROUPS):
            pick = gidx == g
            cv = jnp.where(pick, sel_r[g * EXPERTS_PER_GROUP + j], cv)
            rv = jnp.where(pick, sc_r[g * EXPERTS_PER_GROUP + j], rv)
        cand.append(cv)
        raw.append(rv)

    def argmax4(vals):
        best, idx = vals[0], jnp.zeros_like(gidx)
        for j in range(1, EXPERTS_PER_GROUP):
            upd = vals[j] > best
            best = jnp.where(upd, vals[j], best)
            idx = jnp.where(upd, j, idx)
        return idx

    i1 = argmax4(cand)
    i2 = argmax4([jnp.where(i1 == j, -jnp.inf, cand[j]) for j in range(EXPERTS_PER_GROUP)])
    pick_raw = lambda idx: sum(jnp.where(idx == j, raw[j], 0.0) for j in range(EXPERTS_PER_GROUP))
    g1, g2 = pick_raw(i1), pick_raw(i2)
    tot = g1 + g2
    e1 = gidx * EXPERTS_PER_GROUP + i1
    e2 = gidx * EXPERTS_PER_GROUP + i2
    return e1, e2, g1 / tot, g2 / tot


def _out_ln_route_tail(mix, x_ref, g_ref, b_ref, rw_ref, rb_ref, x1_ref, xp_ref, e_ref, gt_ref):
    tm = x_ref.shape[0]
    z = DEEPNORM_ALPHA * x_ref[...] + mix
    x1 = _layer_norm(z, g_ref[...], b_ref[...])
    x1_ref[...] = x1
    _store_packed(xp_ref, tm, x1)
    x_hi = x1.astype(BF16)
    x_lo = (x1 - x_hi.astype(F32)).astype(BF16)
    p_hi = jnp.dot(x_hi, rw_ref[...], preferred_element_type=F32)
    p_lo = jnp.dot(x_lo, rw_ref[:, :LANES], preferred_element_type=F32)
    logits = p_hi[:, :LANES] + (p_hi[:, LANES:] + p_lo)
    logits_t = logits.T[0:N_EXPERTS, :]
    e1, e2, g1, g2 = _route(logits_t, rb_ref[...])
    e_ref[0:1, :] = e1
    e_ref[1:2, :] = e2
    gt_ref[0:1, :] = g1
    gt_ref[1:2, :] = g2


def _out_ln_route_a_kernel(a_ref, w_ref, *rest):
    mix = jnp.dot(a_ref[...], w_ref[...], preferred_element_type=F32)
    _out_ln_route_tail(mix, *rest)


def _out_ln_route_b_kernel(o0_ref, o1_ref, o2_ref, l0_ref, l1_ref, l2_ref, w_ref,
                           x_ref, g_ref, b_ref, rw_ref, rb_ref, x1_ref, xp_ref, e_ref, gt_ref,
                           os_ref, ls_ref):
    tm = x_ref.shape[0]
    merged = []
    for hd in range(HEADS_PER_BRANCH):
        for g, (o_ref, l_ref) in enumerate(((o1_ref, l1_ref), (o2_ref, l2_ref))):
            d = BRANCHES[g + 1][1]
            for r in range(d):
                cols = slice(r * BRANCH_WIDTH + hd * HEAD_DIM, r * BRANCH_WIDTH + (hd + 1) * HEAD_DIM)
                os_ref.at[g, hd][pl.ds(r, tm // d, stride=d), :] = o_ref[:, cols].astype(F32)
                ls_ref.at[g, hd][pl.ds(r, tm // d, stride=d), :] = l_ref[:, cols]
        cols = slice(hd * HEAD_DIM, (hd + 1) * HEAD_DIM)
        l0, l1, l2 = l0_ref[:, cols], ls_ref[0, hd], ls_ref[1, hd]
        lm = jnp.maximum(jnp.maximum(l0, l1), l2)
        w0, w1, w2 = jnp.exp(l0 - lm), jnp.exp(l1 - lm), jnp.exp(l2 - lm)
        m = (o0_ref[:, cols].astype(F32) * w0 + os_ref[0, hd] * w1 + os_ref[1, hd] * w2) / (w0 + w1 + w2)
        merged.append(m.astype(BF16))
    mix = jnp.dot(jnp.concatenate(merged, axis=1), w_ref[...], preferred_element_type=F32)
    _out_ln_route_tail(mix, x_ref, g_ref, b_ref, rw_ref, rb_ref, x1_ref, xp_ref, e_ref, gt_ref)


def _out_ln_route(lhs, w_bf, x, ln_g, ln_b, rw_split, rb_col):
    tm = OUT_TM
    kdim = w_bf.shape[0]
    row = lambda width: pl.BlockSpec((tm, width), lambda i: (i, 0))
    full = lambda shape: pl.BlockSpec(shape, lambda i: (0,) * len(shape))
    lane_row = pl.BlockSpec((TOP_K, tm), lambda i: (0, i))
    if len(lhs) == 1:
        body, name, lhs_specs, scratch = _out_ln_route_a_kernel, "out_ln_route_conv", [row(kdim)], []
    else:
        body, name = _out_ln_route_b_kernel, "out_ln_route_attn"
        blocked = [pl.BlockSpec((tm // d, d * BRANCH_WIDTH), lambda i: (i, 0)) for _, d in BRANCHES]
        lhs_specs = blocked + blocked
        scratch = [pltpu.VMEM((N_BRANCH - 1, HEADS_PER_BRANCH, tm, HEAD_DIM), F32)] * 2
    return pl.pallas_call(
        body,
        out_shape=(jax.ShapeDtypeStruct((N_TOK, D_MODEL), F32),
                   jax.ShapeDtypeStruct((N_TOK * PACK_ROWS, LANES), U32),
                   jax.ShapeDtypeStruct((TOP_K, N_TOK), jnp.int32),
                   jax.ShapeDtypeStruct((TOP_K, N_TOK), F32)),
        grid=(N_TOK // tm,),
        in_specs=lhs_specs + [
            full((kdim, D_MODEL)), row(D_MODEL), full((1, D_MODEL)), full((1, D_MODEL)),
            full((D_MODEL, 2 * LANES)), full((N_EXPERTS, 1)),
        ],
        out_specs=(row(D_MODEL), pl.BlockSpec((tm * PACK_ROWS, LANES), lambda i: (i, 0)), lane_row, lane_row),
        scratch_shapes=scratch,
        compiler_params=_params(1),
        name=name,
    )(*lhs, w_bf, x, ln_g.reshape(1, D_MODEL), ln_b.reshape(1, D_MODEL), rw_split, rb_col)


def _build_plan(eidx, gates):
    flat_e = eidx.reshape(-1)
    onehot = (flat_e[:, None] == jnp.arange(N_EXPERTS, dtype=jnp.int32)[None, :]).astype(jnp.int32)
    csum = jnp.cumsum(onehot, axis=0)
    rank = jnp.sum(onehot * csum, axis=1) - 1
    counts = csum[-1]
    nblk_e = ((counts + MOE_BLK - 1) // MOE_BLK).astype(jnp.int32)
    blk_end = jnp.cumsum(nblk_e).astype(jnp.int32)
    blk_start = blk_end - nblk_e
    dest = (blk_start[flat_e] * MOE_BLK + rank).astype(jnp.int32)
    n_valid = blk_end[-1:]
    blk_ids = jnp.arange(MOE_NBLK, dtype=jnp.int32)
    block_e = jnp.clip(jnp.searchsorted(blk_end, blk_ids, side="right"), 0, N_EXPERTS - 1).astype(jnp.int32)
    block_valid = jnp.clip(counts[block_e] - (blk_ids - blk_start[block_e]) * MOE_BLK, 0, MOE_BLK).astype(jnp.int32)
    return dest[:N_TOK], dest[N_TOK:], blk_end - 1, nblk_e, block_e, block_valid, n_valid, gates.T


def _dispatch_kernel(d0_ref, d1_ref, last_ref, nblk_ref, xp_ref, xs_hbm, zero_ref, zsem, sem):
    i = pl.program_id(0)
    tm = xp_ref.shape[0] // PACK_ROWS
    blk_rows = MOE_BLK * PACK_ROWS

    @pl.when(i == 0)
    def _():
        zero_ref[...] = jnp.zeros_like(zero_ref)

        def zero_copy(e):
            start = pl.multiple_of(last_ref[e] * blk_rows, blk_rows)
            return pltpu.make_async_copy(zero_ref, xs_hbm.at[pl.ds(start, blk_rows)], zsem)

        for e in range(N_EXPERTS):
            @pl.when(nblk_ref[e] > 0)
            def _():
                zero_copy(e).start()
        for e in range(N_EXPERTS):
            @pl.when(nblk_ref[e] > 0)
            def _():
                zero_copy(e).wait()

    def body(r, carry):
        t = i * tm + r
        src = xp_ref.at[pl.ds(pl.multiple_of(r * PACK_ROWS, PACK_ROWS), PACK_ROWS)]
        for d_ref in (d0_ref, d1_ref):
            dst = xs_hbm.at[pl.ds(pl.multiple_of(d_ref[t] * PACK_ROWS, PACK_ROWS), PACK_ROWS)]
            pltpu.make_async_copy(src, dst, sem).start()
        return carry

    lax.fori_loop(0, tm, body, 0, unroll=DMA_UNROLL)
    for _ in range(TOP_K):
        pltpu.make_async_copy(xp_ref, xs_hbm.at[pl.ds(0, tm * PACK_ROWS)], sem).wait()


def _moe_dispatch(xp, dest0, dest1, last_blk, nblk_e):
    tm = DISPATCH_TM
    return pl.pallas_call(
        _dispatch_kernel,
        out_shape=jax.ShapeDtypeStruct((MOE_ROWS * PACK_ROWS, LANES), U32),
        grid_spec=pltpu.PrefetchScalarGridSpec(
            num_scalar_prefetch=4,
            grid=(N_TOK // tm,),
            in_specs=[pl.BlockSpec((tm * PACK_ROWS, LANES), lambda i, *_: (i, 0))],
            out_specs=pl.BlockSpec(memory_space=pl.ANY),
            scratch_shapes=[pltpu.VMEM((MOE_BLK * PACK_ROWS, LANES), U32),
                            pltpu.SemaphoreType.DMA, pltpu.SemaphoreType.DMA],
        ),
        compiler_params=_params(1),
        name="moe_dispatch",
    )(dest0, dest1, last_blk, nblk_e, xp)


def _moe_ffn_kernel(be_ref, bv_ref, nv_ref, xs_ref, w1_ref, w3_ref, w2_ref, ys_ref,
                    xb_ref, acc_ref, w13_ref, w2b_ref):
    i = pl.program_id(0)
    f = pl.program_id(1)
    blk = MOE_BLK

    @pl.when(i < nv_ref[0])
    def _():
        @pl.when(f == 0)
        def _():
            for c in range(PACK_ROWS):
                lo, hi = _unpack_pair(xs_ref[pl.ds(c, blk, stride=PACK_ROWS), :])
                xb_ref[:, c * LANES:(c + 1) * LANES] = lo.astype(BF16)
                xb_ref[:, HALF + c * LANES:HALF + (c + 1) * LANES] = hi.astype(BF16)
            acc_ref[...] = jnp.zeros_like(acc_ref)

        w13_ref[:, :MOE_TF] = w1_ref[...].astype(BF16)
        w13_ref[:, MOE_TF:] = w3_ref[...].astype(BF16)
        col = f * MOE_TF + lax.broadcasted_iota(jnp.int32, (1, MOE_TF), 1)
        row = f * MOE_TF + lax.broadcasted_iota(jnp.int32, (MOE_TF, 1), 0)
        w2b_ref[...] = jnp.where(row < D_FF, w2_ref[...], 0.0).astype(BF16)
        for s in range(blk // MOE_SUB):
            @pl.when(s * MOE_SUB < bv_ref[i])
            def _():
                rows = slice(s * MOE_SUB, (s + 1) * MOE_SUB)
                h = jnp.dot(xb_ref[rows, :], w13_ref[...], preferred_element_type=F32)
                h1, h3 = h[:, :MOE_TF], h[:, MOE_TF:]
                g = jnp.where(col < D_FF, jax.nn.silu(h1) * h3, 0.0).astype(BF16)
                acc_ref[rows, :] += jnp.dot(g, w2b_ref[...], preferred_element_type=F32)

        @pl.when(f == MOE_NF - 1)
        def _():
            _store_packed(ys_ref, blk, acc_ref[...])


def _moe_ffn(xs, block_e, block_valid, n_valid, w1, w3, w2, layer):
    blk, tf = MOE_BLK, MOE_TF

    def blk_idx(i, nv):
        return jnp.minimum(i, nv[0] - 1)

    def f_idx(i, f, nv):
        return jnp.where(i < nv[0], f, MOE_NF - 1)

    up = pl.BlockSpec((None, None, D_MODEL, tf),
                      lambda i, f, be, bv, nv: (layer, be[blk_idx(i, nv)], 0, f_idx(i, f, nv)))
    down = pl.BlockSpec((None, None, tf, D_MODEL),
                        lambda i, f, be, bv, nv: (layer, be[blk_idx(i, nv)], f_idx(i, f, nv), 0))
    rows = pl.BlockSpec((blk * PACK_ROWS, LANES), lambda i, f, be, bv, nv: (blk_idx(i, nv), 0))
    return pl.pallas_call(
        _moe_ffn_kernel,
        out_shape=jax.ShapeDtypeStruct((MOE_ROWS * PACK_ROWS, LANES), U32),
        grid_spec=pltpu.PrefetchScalarGridSpec(
            num_scalar_prefetch=3,
            grid=(MOE_NBLK, MOE_NF),
            in_specs=[rows, up, up, down],
            out_specs=rows,
            scratch_shapes=[pltpu.VMEM((blk, D_MODEL), BF16), pltpu.VMEM((blk, D_MODEL), F32),
                            pltpu.VMEM((D_MODEL, 2 * tf), BF16), pltpu.VMEM((tf, D_MODEL), BF16)],
        ),
        compiler_params=_params(2),
        name="moe_ffn",
    )(block_e, block_valid, n_valid, xs, w1, w3, w2)


def _combine_ln_kernel(d0_ref, d1_ref, x1_ref, gt_ref, ys_hbm, g_ref, b_ref, x2_ref, x2b_ref,
                       buf_ref, ffn_ref, sems):
    i = pl.program_id(0)
    tm = x1_ref.shape[0]

    def issue(tile):
        slot = tile % 2

        def body(r, carry):
            t = tile * tm + r
            for k, d_ref in enumerate((d0_ref, d1_ref)):
                src = ys_hbm.at[pl.ds(pl.multiple_of(d_ref[t] * PACK_ROWS, PACK_ROWS), PACK_ROWS)]
                dst = buf_ref.at[slot, k, pl.ds(pl.multiple_of(r * PACK_ROWS, PACK_ROWS), PACK_ROWS)]
                pltpu.make_async_copy(src, dst, sems.at[slot]).start()
            return carry

        lax.fori_loop(0, tm, body, 0, unroll=DMA_UNROLL)

    @pl.when(i == 0)
    def _():
        issue(0)

    @pl.when(i + 1 < pl.num_programs(0))
    def _():
        issue(i + 1)

    slot = i % 2
    for k in range(TOP_K):
        pltpu.make_async_copy(ys_hbm.at[pl.ds(0, tm * PACK_ROWS)], buf_ref.at[slot, k], sems.at[slot]).wait()
    gt = gt_ref[...]
    g0, g1 = gt[:, 0:1], gt[:, 1:2]
    for c in range(PACK_ROWS):
        lo0, hi0 = _unpack_pair(buf_ref.at[slot, 0][pl.ds(c, tm, stride=PACK_ROWS), :])
        lo1, hi1 = _unpack_pair(buf_ref.at[slot, 1][pl.ds(c, tm, stride=PACK_ROWS), :])
        ffn_ref[:, c * LANES:(c + 1) * LANES] = lo0 * g0 + lo1 * g1
        ffn_ref[:, HALF + c * LANES:HALF + (c + 1) * LANES] = hi0 * g0 + hi1 * g1
    x2 = _layer_norm(DEEPNORM_ALPHA * x1_ref[...] + ffn_ref[...], g_ref[...], b_ref[...])
    x2_ref[...] = x2
    x2b_ref[...] = x2.astype(BF16)


def _combine_ln(x1, ys, dest0, dest1, gates_t, ln_g, ln_b):
    tm = COMB_TM
    row = lambda width: pl.BlockSpec((tm, width), lambda i, d0, d1: (i, 0))
    vec = pl.BlockSpec((1, D_MODEL), lambda i, d0, d1: (0, 0))
    return pl.pallas_call(
        _combine_ln_kernel,
        out_shape=(jax.ShapeDtypeStruct((N_TOK, D_MODEL), F32),
                   jax.ShapeDtypeStruct((N_TOK, D_MODEL), BF16)),
        grid_spec=pltpu.PrefetchScalarGridSpec(
            num_scalar_prefetch=2,
            grid=(N_TOK // tm,),
            in_specs=[row(D_MODEL), row(TOP_K), pl.BlockSpec(memory_space=pl.ANY), vec, vec],
            out_specs=(row(D_MODEL), row(D_MODEL)),
            scratch_shapes=[pltpu.VMEM((2, TOP_K, tm * PACK_ROWS, LANES), U32),
                            pltpu.VMEM((tm, D_MODEL), F32),
                            pltpu.SemaphoreType.DMA((2,))],
        ),
        compiler_params=_params(1),
        name="combine_ln",
    )(dest0, dest1, x1, gates_t, ys, ln_g.reshape(1, D_MODEL), ln_b.reshape(1, D_MODEL))


def kernel(x, a_w_in, a_conv_w, a_w_out, kv_w, b_w_q, b_w_o, router_w, router_bias,
           moe_w1, moe_w3, moe_w2, ln1_g, ln1_b, ln2_g, ln2_b):
    x = x.reshape(N_TOK, D_MODEL)
    xb = x.astype(BF16)
    rw_pad = jnp.pad(router_w.astype(F32), ((0, 0), (0, LANES - N_EXPERTS)))
    rw_hi = rw_pad.astype(BF16)
    rw_lo = (rw_pad - rw_hi.astype(F32)).astype(BF16)
    rw_split = jnp.concatenate([rw_hi, rw_lo], axis=1)
    rb_col = router_bias.astype(F32).reshape(N_EXPERTS, 1)
    k_tables = _rope_tables(1.0)
    q_tables = _rope_tables(1.0 / math.sqrt(HEAD_DIM))
    dils = [d for _, d in BRANCHES]
    k_sh = v_sh = None
    for i in range(DEPTH):
        if i < N_A_LAYERS:
            y = _conv_proj(xb, a_w_in, a_conv_w, i)
            lhs, w_out = [y], a_w_out[i].astype(BF16)
        else:
            j = i - N_A_LAYERS
            if k_sh is None:
                kv3 = kv_w[None]
                k_sh = [_proj(xb, kv3, 0, g, dils[g], k_tables, rope=True) for g in range(N_BRANCH)]
                v_sh = [_proj(xb, kv3, 0, N_BRANCH + g, dils[g], k_tables, rope=False) for g in range(N_BRANCH)]
            q = [_proj(xb, b_w_q, j, g, dils[g], q_tables, rope=True) for g in range(N_BRANCH)]
            outs = [_attn_branch(q[g], k_sh[g], v_sh[g], dils[g]) for g in range(N_BRANCH)]
            lhs, w_out = [o for o, _ in outs] + [l for _, l in outs], b_w_o[j].astype(BF16)
        x1, xp, eidx, gates = _out_ln_route(lhs, w_out, x, ln1_g[i], ln1_b[i], rw_split, rb_col)
        dest0, dest1, last_blk, nblk_e, block_e, block_valid, n_valid, gates_t = _build_plan(eidx, gates)
        xs = _moe_dispatch(xp, dest0, dest1, last_blk, nblk_e)
        ys = _moe_ffn(xs, block_e, block_valid, n_valid, moe_w1, moe_w3, moe_w2, i)
        x, xb = _combine_ln(x1, ys, dest0, dest1, gates_t, ln2_g[i], ln2_b[i])
    return x.reshape(BATCH, SEQ, D_MODEL)
```

```python
import functools
import math

import jax
import jax.numpy as jnp
from jax import lax
from jax.experimental import pallas as pl
from jax.experimental.pallas import tpu as pltpu

D_MODEL = 2048
BATCH = 4
SEQ = 4096
DEPTH = 4
N_TOK = BATCH * SEQ
N_A_LAYERS = DEPTH // 2
CONV_WIDTH = 3
BRANCHES = ((128, 1), (512, 4), (2048, 16))
N_BRANCH = len(BRANCHES)
HEADS_PER_BRANCH = 8
HEAD_DIM = 128
BRANCH_WIDTH = HEADS_PER_BRANCH * HEAD_DIM
SPAN = 128
ROT_DIM = HEAD_DIM // 4
ROPE_THETA = 500000.0
N_EXPERTS = 16
N_GROUPS = 4
EXPERTS_PER_GROUP = N_EXPERTS // N_GROUPS
TOP_K = 2
D_FF = 1408
DEEPNORM_ALPHA = (2.0 * DEPTH) ** 0.25
LN_EPS = 1e-5

F32 = jnp.float32
BF16 = jnp.bfloat16
U32 = jnp.uint32

VMEM_LIMIT_BYTES = 56 * 1024 * 1024
LANES = 128
HALF = D_MODEL // 2
PACK_ROWS = HALF // LANES

CONV_TM, CONV_TN = 1024, 256
PROJ_TM = 512
OUT_TM = 256
ATTN_QB = 256
MOE_BLK = 256
MOE_NBLK = (N_TOK * TOP_K) // MOE_BLK + N_EXPERTS
MOE_ROWS = MOE_NBLK * MOE_BLK
MOE_CHUNKS = 8
MOE_UP_ROWS = D_MODEL // MOE_CHUNKS
MOE_DOWN_ROWS = D_FF // MOE_CHUNKS
MOE_STEPS = MOE_NBLK + N_EXPERTS * MOE_CHUNKS
DISPATCH_TM = 512
COMB_TM = 512
DMA_UNROLL = 8


def _params(n_axes):
    return pltpu.CompilerParams(
        dimension_semantics=("arbitrary",) * n_axes,
        vmem_limit_bytes=VMEM_LIMIT_BYTES,
    )


def _pack_pair(lo, hi):
    def rne(v):
        b = lax.bitcast_convert_type(v, U32)
        return (b + U32(0x7FFF) + ((b >> 16) & U32(1))) >> 16
    return rne(lo) | (rne(hi) << 16)


def _unpack_pair(w):
    lo = lax.bitcast_convert_type(w << 16, F32)
    hi = lax.bitcast_convert_type(w & U32(0xFFFF0000), F32)
    return lo, hi


def _store_packed(dst_ref, rows, v):
    packed = _pack_pair(v[:, :HALF], v[:, HALF:])
    for c in range(PACK_ROWS):
        dst_ref[pl.ds(c, rows, stride=PACK_ROWS), :] = packed[:, c * LANES:(c + 1) * LANES]


def _conv_proj_kernel(x_ref, wb_ref, wc_ref, wh_ref, cw_ref, y_ref, wbf_ref, ext_ref):
    i = pl.program_id(1)
    tm = x_ref.shape[0]

    @pl.when(i == 0)
    def _():
        wbf_ref[0] = wb_ref[...].astype(BF16)
        wbf_ref[1] = wc_ref[...].astype(BF16)
        wbf_ref[2] = wh_ref[...].astype(BF16)

    @pl.when(i % (SEQ // tm) == 0)
    def _():
        ext_ref[0:8, :] = jnp.zeros((8, ext_ref.shape[1]), F32)

    x = x_ref[...]
    gate_b = jnp.dot(x, wbf_ref[0], preferred_element_type=F32)
    gate_c = jnp.dot(x, wbf_ref[1], preferred_element_type=F32)
    h = jnp.dot(x, wbf_ref[2], preferred_element_type=F32)
    u = gate_c * h
    ext_ref[8:8 + tm, :] = u
    u1 = ext_ref[7:7 + tm, :]
    u2 = ext_ref[6:6 + tm, :]
    cw = cw_ref[...]
    conv = cw[2:3, :] * u + cw[1:2, :] * u1 + cw[0:1, :] * u2
    y_ref[...] = (gate_b * conv).astype(BF16)
    ext_ref[0:8, :] = ext_ref[tm:tm + 8, :]


def _conv_proj(xb, w_in, conv_w, layer):
    tm, tn = CONV_TM, CONV_TN
    nj = D_MODEL // tn
    wspec = lambda off: pl.BlockSpec((None, D_MODEL, tn), lambda j, i: (layer, 0, j + off * nj))
    return pl.pallas_call(
        _conv_proj_kernel,
        out_shape=jax.ShapeDtypeStruct((N_TOK, D_MODEL), BF16),
        grid=(nj, N_TOK // tm),
        in_specs=[
            pl.BlockSpec((tm, D_MODEL), lambda j, i: (i, 0)),
            wspec(0), wspec(1), wspec(2),
            pl.BlockSpec((None, CONV_WIDTH, tn), lambda j, i: (layer, 0, j)),
        ],
        out_specs=pl.BlockSpec((tm, tn), lambda j, i: (i, j)),
        scratch_shapes=[pltpu.VMEM((3, D_MODEL, tn), BF16), pltpu.VMEM((tm + 8, tn), F32)],
        compiler_params=_params(2),
        name="conv_proj",
    )(xb, w_in, w_in, w_in, conv_w)


def _proj_kernel(x_ref, w_ref, c_ref, s1_ref, s2_ref, o_ref, wbf_ref, rot_ref, *, rope, dilation):
    tm = x_ref.shape[0]

    @pl.when(pl.program_id(0) == 0)
    def _():
        wbf_ref[...] = w_ref[...].astype(BF16)

    acc = jnp.dot(x_ref[...], wbf_ref[...], preferred_element_type=F32)
    if rope:
        c = c_ref[...]
        s1 = s1_ref[...]
        s2 = s2_ref[...]
    for hd in range(HEADS_PER_BRANCH):
        sl = slice(hd * HEAD_DIM, (hd + 1) * HEAD_DIM)
        t = acc[:, sl]
        if rope:
            t = t * c + pltpu.roll(t, HEAD_DIM - ROT_DIM // 2, 1) * s1 + pltpu.roll(t, ROT_DIM // 2, 1) * s2
        if dilation == 1:
            o_ref[:, sl] = t.astype(BF16)
        else:
            rot_ref[hd] = t
            for r in range(dilation):
                piece = rot_ref.at[hd][pl.ds(r, tm // dilation, stride=dilation), :]
                o_ref[:, r * BRANCH_WIDTH + hd * HEAD_DIM:r * BRANCH_WIDTH + (hd + 1) * HEAD_DIM] = piece.astype(BF16)


def _proj(xb, w3d, layer, col, dilation, tables, rope):
    tm = PROJ_TM
    w = BRANCH_WIDTH
    tab = pl.BlockSpec((tm, HEAD_DIM), lambda i: (i % (SEQ // tm), 0))
    return pl.pallas_call(
        functools.partial(_proj_kernel, rope=rope, dilation=dilation),
        out_shape=jax.ShapeDtypeStruct((N_TOK // dilation, dilation * w), BF16),
        grid=(N_TOK // tm,),
        in_specs=[
            pl.BlockSpec((tm, D_MODEL), lambda i: (i, 0)),
            pl.BlockSpec((None, D_MODEL, w), lambda i: (layer, 0, col)),
            tab, tab, tab,
        ],
        out_specs=pl.BlockSpec((tm // dilation, dilation * w), lambda i: (i, 0)),
        scratch_shapes=[pltpu.VMEM((D_MODEL, w), BF16), pltpu.VMEM((HEADS_PER_BRANCH, tm, HEAD_DIM), F32)],
        compiler_params=_params(1),
        name=("proj_rope" if rope else "proj") + f"_d{dilation}",
    )(xb, w3d, *tables)


def _rope_tables(scale):
    inv_freq = ROPE_THETA ** (-jnp.arange(0, ROT_DIM, 2, dtype=F32) / ROT_DIM)
    ang = jnp.arange(SEQ, dtype=F32)[:, None] * inv_freq[None, :]
    cos, sin = jnp.cos(ang), jnp.sin(ang)
    half = ROT_DIM // 2
    rest = HEAD_DIM - ROT_DIM
    c = jnp.concatenate([cos, cos, jnp.ones((SEQ, rest), F32)], axis=1) * scale
    s1 = jnp.concatenate([-sin, jnp.zeros((SEQ, HEAD_DIM - half), F32)], axis=1) * scale
    s2 = jnp.concatenate([jnp.zeros((SEQ, half), F32), sin, jnp.zeros((SEQ, rest), F32)], axis=1) * scale
    return c, s1, s2


def _attn_kernel(q_ref, kc_ref, kp_ref, vc_ref, vp_ref, o_ref, l_ref):
    qb = q_ref.shape[0]
    qi = lax.broadcasted_iota(jnp.int32, (SPAN, 2 * SPAN), 0)
    kj = lax.broadcasted_iota(jnp.int32, (SPAN, 2 * SPAN), 1)
    dist = SPAN + qi - kj
    band = (dist >= 0) & (dist <= SPAN)
    first_key = jnp.where(pl.program_id(1) == 0, SPAN, 0)
    band_first = band & (kj >= first_key)
    for a in range(qb // SPAN):
        rows = slice(a * SPAN, (a + 1) * SPAN)
        for hd in range(HEADS_PER_BRANCH):
            cols = slice(hd * HEAD_DIM, (hd + 1) * HEAD_DIM)
            q = q_ref[rows, cols]
            if a == 0:
                k_prev, v_prev = kp_ref[:, cols], vp_ref[:, cols]
                mask = band_first
            else:
                prows = slice((a - 1) * SPAN, a * SPAN)
                k_prev, v_prev = kc_ref[prows, cols], vc_ref[prows, cols]
                mask = band
            kk = jnp.concatenate([k_prev, kc_ref[rows, cols]], axis=0)
            vv = jnp.concatenate([v_prev, vc_ref[rows, cols]], axis=0)
            s = lax.dot_general(q, kk, (((1,), (1,)), ((), ())), preferred_element_type=F32)
            s = jnp.where(mask, s, -jnp.inf)
            m = jnp.max(s, axis=-1, keepdims=True)
            p = jnp.exp(s - m)
            den = jnp.sum(p, axis=-1, keepdims=True)
            o = jnp.dot(p.astype(BF16), vv, preferred_element_type=F32) / den
            o_ref[rows, cols] = o.astype(BF16)
            l_ref[rows, cols] = jnp.broadcast_to(m + jnp.log(den), (SPAN, HEAD_DIM))


def _attn_branch(q, k, v, dilation):
    sub = SEQ // dilation
    qb = min(ATTN_QB, sub)
    w = BRANCH_WIDTH
    view = lambda t: t.reshape(BATCH, sub, dilation * w)
    cur = pl.BlockSpec((None, qb, w), lambda b, n, r: (b, n, r))
    prev = pl.BlockSpec((None, SPAN, w),
                        lambda b, n, r: (b, jnp.maximum(n * (qb // SPAN) - 1, 0), r))
    o, l = pl.pallas_call(
        _attn_kernel,
        out_shape=(jax.ShapeDtypeStruct((BATCH, sub, dilation * w), BF16),
                   jax.ShapeDtypeStruct((BATCH, sub, dilation * w), F32)),
        grid=(BATCH, sub // qb, dilation),
        in_specs=[cur, cur, prev, cur, prev],
        out_specs=(cur, cur),
        compiler_params=_params(3),
        name=f"dilated_attn_d{dilation}",
    )(view(q), view(k), view(k), view(v), view(v))
    return o.reshape(N_TOK // dilation, dilation * w), l.reshape(N_TOK // dilation, dilation * w)


def _layer_norm(z, g, b):
    mean = jnp.mean(z, axis=-1, keepdims=True)
    zc = z - mean
    var = jnp.mean(zc * zc, axis=-1, keepdims=True)
    return zc * lax.rsqrt(var + LN_EPS) * g + b


def _route(logits_t, bias_col):
    scores = jax.nn.sigmoid(logits_t)
    sel = scores + bias_col
    sel_r = [sel[e:e + 1, :] for e in range(N_EXPERTS)]
    sc_r = [scores[e:e + 1, :] for e in range(N_EXPERTS)]
    best_g = None
    for g in range(N_GROUPS):
        v = sel_r[g * EXPERTS_PER_GROUP:(g + 1) * EXPERTS_PER_GROUP]
        gs = None
        for a in range(EXPERTS_PER_GROUP):
            for b in range(a + 1, EXPERTS_PER_GROUP):
                ps = v[a] + v[b]
                gs = ps if gs is None else jnp.maximum(gs, ps)
        if best_g is None:
            best_g, gidx = gs, jnp.zeros_like(gs, dtype=jnp.int32)
        else:
            upd = gs > best_g
            best_g = jnp.where(upd, gs, best_g)
            gidx = jnp.where(upd, g, gidx)
    cand, raw = [], []
    for j in range(EXPERTS_PER_GROUP):
        cv, rv = sel_r[j], sc_r[j]
        for g in range(1, N_GROUPS):
            pick = gidx == g
            cv = jnp.where(pick, sel_r[g * EXPERTS_PER_GROUP + j], cv)
            rv = jnp.where(pick, sc_r[g * EXPERTS_PER_GROUP + j], rv)
        cand.append(cv)
        raw.append(rv)

    def argmax4(vals):
        best, idx = vals[0], jnp.zeros_like(gidx)
        for j in range(1, EXPERTS_PER_GROUP):
            upd = vals[j] > best
            best = jnp.where(upd, vals[j], best)
            idx = jnp.where(upd, j, idx)
        return idx

    i1 = argmax4(cand)
    i2 = argmax4([jnp.where(i1 == j, -jnp.inf, cand[j]) for j in range(EXPERTS_PER_GROUP)])
    pick_raw = lambda idx: sum(jnp.where(idx == j, raw[j], 0.0) for j in range(EXPERTS_PER_GROUP))
    g1, g2 = pick_raw(i1), pick_raw(i2)
    tot = g1 + g2
    e1 = gidx * EXPERTS_PER_GROUP + i1
    e2 = gidx * EXPERTS_PER_GROUP + i2
    return e1, e2, g1 / tot, g2 / tot


def _out_ln_route_tail(mix, x_ref, g_ref, b_ref, rw_ref, rb_ref, x1_ref, xp_ref, e_ref, gt_ref):
    tm = x_ref.shape[0]
    z = DEEPNORM_ALPHA * x_ref[...] + mix
    x1 = _layer_norm(z, g_ref[...], b_ref[...])
    x1_ref[...] = x1
    _store_packed(xp_ref, tm, x1)
    x_hi = x1.astype(BF16)
    x_lo = (x1 - x_hi.astype(F32)).astype(BF16)
    p_hi = jnp.dot(x_hi, rw_ref[...], preferred_element_type=F32)
    p_lo = jnp.dot(x_lo, rw_ref[:, :LANES], preferred_element_type=F32)
    logits = p_hi[:, :LANES] + (p_hi[:, LANES:] + p_lo)
    logits_t = logits.T[0:N_EXPERTS, :]
    e1, e2, g1, g2 = _route(logits_t, rb_ref[...])
    e_ref[0:1, :] = e1
    e_ref[1:2, :] = e2
    gt_ref[0:1, :] = g1
    gt_ref[1:2, :] = g2


def _out_ln_route_a_kernel(a_ref, w_ref, *rest):
    mix = jnp.dot(a_ref[...], w_ref[...], preferred_element_type=F32)
    _out_ln_route_tail(mix, *rest)


def _out_ln_route_b_kernel(o0_ref, o1_ref, o2_ref, l0_ref, l1_ref, l2_ref, w_ref,
                           x_ref, g_ref, b_ref, rw_ref, rb_ref, x1_ref, xp_ref, e_ref, gt_ref,
                           os_ref, ls_ref):
    tm = x_ref.shape[0]
    merged = []
    for hd in range(HEADS_PER_BRANCH):
        for g, (o_ref, l_ref) in enumerate(((o1_ref, l1_ref), (o2_ref, l2_ref))):
            d = BRANCHES[g + 1][1]
            for r in range(d):
                cols = slice(r * BRANCH_WIDTH + hd * HEAD_DIM, r * BRANCH_WIDTH + (hd + 1) * HEAD_DIM)
                os_ref.at[g, hd][pl.ds(r, tm // d, stride=d), :] = o_ref[:, cols].astype(F32)
                ls_ref.at[g, hd][pl.ds(r, tm // d, stride=d), :] = l_ref[:, cols]
        cols = slice(hd * HEAD_DIM, (hd + 1) * HEAD_DIM)
        l0, l1, l2 = l0_ref[:, cols], ls_ref[0, hd], ls_ref[1, hd]
        lm = jnp.maximum(jnp.maximum(l0, l1), l2)
        w0, w1, w2 = jnp.exp(l0 - lm), jnp.exp(l1 - lm), jnp.exp(l2 - lm)
        m = (o0_ref[:, cols].astype(F32) * w0 + os_ref[0, hd] * w1 + os_ref[1, hd] * w2) / (w0 + w1 + w2)
        merged.append(m.astype(BF16))
    mix = jnp.dot(jnp.concatenate(merged, axis=1), w_ref[...], preferred_element_type=F32)
    _out_ln_route_tail(mix, x_ref, g_ref, b_ref, rw_ref, rb_ref, x1_ref, xp_ref, e_ref, gt_ref)


def _out_ln_route(lhs, w_bf, x, ln_g, ln_b, rw_split, rb_col):
    tm = OUT_TM
    kdim = w_bf.shape[0]
    row = lambda width: pl.BlockSpec((tm, width), lambda i: (i, 0))
    full = lambda shape: pl.BlockSpec(shape, lambda i: (0,) * len(shape))
    lane_row = pl.BlockSpec((TOP_K, tm), lambda i: (0, i))
    if len(lhs) == 1:
        body, name, lhs_specs, scratch = _out_ln_route_a_kernel, "out_ln_route_conv", [row(kdim)], []
    else:
        body, name = _out_ln_route_b_kernel, "out_ln_route_attn"
        blocked = [pl.BlockSpec((tm // d, d * BRANCH_WIDTH), lambda i: (i, 0)) for _, d in BRANCHES]
        lhs_specs = blocked + blocked
        scratch = [pltpu.VMEM((N_BRANCH - 1, HEADS_PER_BRANCH, tm, HEAD_DIM), F32)] * 2
    return pl.pallas_call(
        body,
        out_shape=(jax.ShapeDtypeStruct((N_TOK, D_MODEL), F32),
                   jax.ShapeDtypeStruct((N_TOK * PACK_ROWS, LANES), U32),
                   jax.ShapeDtypeStruct((TOP_K, N_TOK), jnp.int32),
                   jax.ShapeDtypeStruct((TOP_K, N_TOK), F32)),
        grid=(N_TOK // tm,),
        in_specs=lhs_specs + [
            full((kdim, D_MODEL)), row(D_MODEL), full((1, D_MODEL)), full((1, D_MODEL)),
            full((D_MODEL, 2 * LANES)), full((N_EXPERTS, 1)),
        ],
        out_specs=(row(D_MODEL), pl.BlockSpec((tm * PACK_ROWS, LANES), lambda i: (i, 0)), lane_row, lane_row),
        scratch_shapes=scratch,
        compiler_params=_params(1),
        name=name,
    )(*lhs, w_bf, x, ln_g.reshape(1, D_MODEL), ln_b.reshape(1, D_MODEL), rw_split, rb_col)


def _build_plan(eidx, gates):
    flat_e = eidx.reshape(-1)
    onehot = (flat_e[:, None] == jnp.arange(N_EXPERTS, dtype=jnp.int32)[None, :]).astype(jnp.int32)
    csum = jnp.cumsum(onehot, axis=0)
    rank = jnp.sum(onehot * csum, axis=1) - 1
    counts = csum[-1]
    nblk_e = ((counts + MOE_BLK - 1) // MOE_BLK).astype(jnp.int32)
    blk_end = jnp.cumsum(nblk_e).astype(jnp.int32)
    blk_start = blk_end - nblk_e
    dest = (blk_start[flat_e] * MOE_BLK + rank).astype(jnp.int32)
    n_valid = blk_end[-1]

    chunks = jnp.full((1,), MOE_CHUNKS, jnp.int32)
    phase_len = jnp.concatenate([chunks, jnp.maximum(nblk_e[:-1], MOE_CHUNKS), nblk_e[-1:]])
    phase_end = jnp.cumsum(phase_len).astype(jnp.int32)
    phase_start = phase_end - phase_len
    step = jnp.arange(MOE_STEPS, dtype=jnp.int32)
    ph = jnp.clip(jnp.searchsorted(phase_end, step, side="right"), 0, N_EXPERTS).astype(jnp.int32)
    off = step - phase_start[ph]
    live = step < phase_end[-1]
    ce = jnp.clip(ph - 1, 0, N_EXPERTS - 1)
    nb_c = jnp.where(ph >= 1, nblk_e[ce], 0)
    comp_on = (live & (ph >= 1) & (off < nb_c)).astype(jnp.int32)
    comp_blk = jnp.clip(jnp.where(ph >= 1, blk_start[ce], 0) + jnp.minimum(off, nb_c - 1), 0, n_valid - 1)
    load_on = (live & (ph < N_EXPERTS) & (off < MOE_CHUNKS)).astype(jnp.int32)
    load_e = jnp.minimum(ph, N_EXPERTS - 1)
    load_c = jnp.where(ph < N_EXPERTS, jnp.minimum(off, MOE_CHUNKS - 1), MOE_CHUNKS - 1)
    sched = (comp_on, comp_blk.astype(jnp.int32), (ph + 1) % 2, load_on, load_e, load_c.astype(jnp.int32), ph % 2)
    return dest[:N_TOK], dest[N_TOK:], blk_end - 1, nblk_e, sched, gates.T


def _dispatch_kernel(d0_ref, d1_ref, last_ref, nblk_ref, xp_ref, xs_hbm, zero_ref, zsem, sem):
    i = pl.program_id(0)
    tm = xp_ref.shape[0] // PACK_ROWS
    blk_rows = MOE_BLK * PACK_ROWS

    @pl.when(i == 0)
    def _():
        zero_ref[...] = jnp.zeros_like(zero_ref)

        def zero_copy(e):
            start = pl.multiple_of(last_ref[e] * blk_rows, blk_rows)
            return pltpu.make_async_copy(zero_ref, xs_hbm.at[pl.ds(start, blk_rows)], zsem)

        for e in range(N_EXPERTS):
            @pl.when(nblk_ref[e] > 0)
            def _():
                zero_copy(e).start()
        for e in range(N_EXPERTS):
            @pl.when(nblk_ref[e] > 0)
            def _():
                zero_copy(e).wait()

    def body(r, carry):
        t = i * tm + r
        src = xp_ref.at[pl.ds(pl.multiple_of(r * PACK_ROWS, PACK_ROWS), PACK_ROWS)]
        for d_ref in (d0_ref, d1_ref):
            dst = xs_hbm.at[pl.ds(pl.multiple_of(d_ref[t] * PACK_ROWS, PACK_ROWS), PACK_ROWS)]
            pltpu.make_async_copy(src, dst, sem).start()
        return carry

    lax.fori_loop(0, tm, body, 0, unroll=DMA_UNROLL)
    for _ in range(TOP_K):
        pltpu.make_async_copy(xp_ref, xs_hbm.at[pl.ds(0, tm * PACK_ROWS)], sem).wait()


def _moe_dispatch(xp, dest0, dest1, last_blk, nblk_e):
    tm = DISPATCH_TM
    return pl.pallas_call(
        _dispatch_kernel,
        out_shape=jax.ShapeDtypeStruct((MOE_ROWS * PACK_ROWS, LANES), U32),
        grid_spec=pltpu.PrefetchScalarGridSpec(
            num_scalar_prefetch=4,
            grid=(N_TOK // tm,),
            in_specs=[pl.BlockSpec((tm * PACK_ROWS, LANES), lambda i, *_: (i, 0))],
            out_specs=pl.BlockSpec(memory_space=pl.ANY),
            scratch_shapes=[pltpu.VMEM((MOE_BLK * PACK_ROWS, LANES), U32),
                            pltpu.SemaphoreType.DMA, pltpu.SemaphoreType.DMA],
        ),
        compiler_params=_params(1),
        name="moe_dispatch",
    )(dest0, dest1, last_blk, nblk_e, xp)


def _moe_ffn_kernel(con_ref, cblk_ref, cslot_ref, lon_ref, le_ref, lc_ref, lslot_ref,
                    xs_ref, w1_ref, w3_ref, w2_ref, ys_ref, w13_ref, w2b_ref):
    s = pl.program_id(0)
    blk = MOE_BLK
    n_ff = D_FF // LANES

    @pl.when(lon_ref[s] == 1)
    def _():
        slot = lslot_ref[s]
        rows = pl.ds(pl.multiple_of(lc_ref[s] * MOE_UP_ROWS, MOE_UP_ROWS), MOE_UP_ROWS)
        for j in range(n_ff):
            src = slice(j * LANES, (j + 1) * LANES)
            w13_ref[slot, rows, 2 * j * LANES:(2 * j + 1) * LANES] = w1_ref[:, src].astype(BF16)
            w13_ref[slot, rows, (2 * j + 1) * LANES:(2 * j + 2) * LANES] = w3_ref[:, src].astype(BF16)
        rows2 = pl.ds(pl.multiple_of(lc_ref[s] * MOE_DOWN_ROWS, 16), MOE_DOWN_ROWS)
        w2b_ref[slot, rows2, :] = w2_ref[...].astype(BF16)

    @pl.when(con_ref[s] == 1)
    def _():
        slot = cslot_ref[s]
        pieces = [_unpack_pair(xs_ref[pl.ds(c, blk, stride=PACK_ROWS), :]) for c in range(PACK_ROWS)]
        x = jnp.concatenate([lo.astype(BF16) for lo, _ in pieces] + [hi.astype(BF16) for _, hi in pieces], axis=1)
        acc = None
        for q in range(0, n_ff, 2):
            npair = min(2, n_ff - q)
            h = jnp.dot(x, w13_ref[slot, :, 2 * q * LANES:2 * (q + npair) * LANES], preferred_element_type=F32)
            g = [jax.nn.silu(h[:, 2 * k * LANES:(2 * k + 1) * LANES]) * h[:, (2 * k + 1) * LANES:(2 * k + 2) * LANES]
                 for k in range(npair)]
            g = (g[0] if npair == 1 else jnp.concatenate(g, axis=1)).astype(BF16)
            part = jnp.dot(g, w2b_ref[slot, q * LANES:(q + npair) * LANES, :], preferred_element_type=F32)
            acc = part if acc is None else acc + part
        _store_packed(ys_ref, blk, acc)


def _moe_ffn(xs, sched, w1, w3, w2, layer):
    blk = MOE_BLK
    rows = pl.BlockSpec((blk * PACK_ROWS, LANES), lambda s, con, cblk, *_: (cblk[s], 0))
    up = pl.BlockSpec((None, None, MOE_UP_ROWS, D_FF),
                      lambda s, con, cblk, cslot, lon, le, lc, lslot: (layer, le[s], lc[s], 0))
    down = pl.BlockSpec((None, None, MOE_DOWN_ROWS, D_MODEL),
                        lambda s, con, cblk, cslot, lon, le, lc, lslot: (layer, le[s], lc[s], 0))
    return pl.pallas_call(
        _moe_ffn_kernel,
        out_shape=jax.ShapeDtypeStruct((MOE_ROWS * PACK_ROWS, LANES), U32),
        grid_spec=pltpu.PrefetchScalarGridSpec(
            num_scalar_prefetch=len(sched),
            grid=(MOE_STEPS,),
            in_specs=[rows, up, up, down],
            out_specs=rows,
            scratch_shapes=[pltpu.VMEM((2, D_MODEL, 2 * D_FF), BF16), pltpu.VMEM((2, D_FF, D_MODEL), BF16)],
        ),
        compiler_params=_params(1),
        name="moe_ffn",
    )(*sched, xs, w1, w3, w2)


def _combine_ln_kernel(d0_ref, d1_ref, x1_ref, gt_ref, ys_hbm, g_ref, b_ref, x2_ref, x2b_ref,
                       buf_ref, ffn_ref, sems):
    i = pl.program_id(0)
    tm = x1_ref.shape[0]

    def issue(tile):
        slot = tile % 2

        def body(r, carry):
            t = tile * tm + r
            for k, d_ref in enumerate((d0_ref, d1_ref)):
                src = ys_hbm.at[pl.ds(pl.multiple_of(d_ref[t] * PACK_ROWS, PACK_ROWS), PACK_ROWS)]
                dst = buf_ref.at[slot, k, pl.ds(pl.multiple_of(r * PACK_ROWS, PACK_ROWS), PACK_ROWS)]
                pltpu.make_async_copy(src, dst, sems.at[slot]).start()
            return carry

        lax.fori_loop(0, tm, body, 0, unroll=DMA_UNROLL)

    @pl.when(i == 0)
    def _():
        issue(0)

    @pl.when(i + 1 < pl.num_programs(0))
    def _():
        issue(i + 1)

    slot = i % 2
    for k in range(TOP_K):
        pltpu.make_async_copy(ys_hbm.at[pl.ds(0, tm * PACK_ROWS)], buf_ref.at[slot, k], sems.at[slot]).wait()
    gt = gt_ref[...]
    g0, g1 = gt[:, 0:1], gt[:, 1:2]
    for c in range(PACK_ROWS):
        lo0, hi0 = _unpack_pair(buf_ref.at[slot, 0][pl.ds(c, tm, stride=PACK_ROWS), :])
        lo1, hi1 = _unpack_pair(buf_ref.at[slot, 1][pl.ds(c, tm, stride=PACK_ROWS), :])
        ffn_ref[:, c * LANES:(c + 1) * LANES] = lo0 * g0 + lo1 * g1
        ffn_ref[:, HALF + c * LANES:HALF + (c + 1) * LANES] = hi0 * g0 + hi1 * g1
    x2 = _layer_norm(DEEPNORM_ALPHA * x1_ref[...] + ffn_ref[...], g_ref[...], b_ref[...])
    x2_ref[...] = x2
    x2b_ref[...] = x2.astype(BF16)


def _combine_ln(x1, ys, dest0, dest1, gates_t, ln_g, ln_b):
    tm = COMB_TM
    row = lambda width: pl.BlockSpec((tm, width), lambda i, d0, d1: (i, 0))
    vec = pl.BlockSpec((1, D_MODEL), lambda i, d0, d1: (0, 0))
    return pl.pallas_call(
        _combine_ln_kernel,
        out_shape=(jax.ShapeDtypeStruct((N_TOK, D_MODEL), F32),
                   jax.ShapeDtypeStruct((N_TOK, D_MODEL), BF16)),
        grid_spec=pltpu.PrefetchScalarGridSpec(
            num_scalar_prefetch=2,
            grid=(N_TOK // tm,),
            in_specs=[row(D_MODEL), row(TOP_K), pl.BlockSpec(memory_space=pl.ANY), vec, vec],
            out_specs=(row(D_MODEL), row(D_MODEL)),
            scratch_shapes=[pltpu.VMEM((2, TOP_K, tm * PACK_ROWS, LANES), U32),
                            pltpu.VMEM((tm, D_MODEL), F32),
                            pltpu.SemaphoreType.DMA((2,))],
        ),
        compiler_params=_params(1),
        name="combine_ln",
    )(dest0, dest1, x1, gates_t, ys, ln_g.reshape(1, D_MODEL), ln_b.reshape(1, D_MODEL))


def kernel(x, a_w_in, a_conv_w, a_w_out, kv_w, b_w_q, b_w_o, router_w, router_bias,
           moe_w1, moe_w3, moe_w2, ln1_g, ln1_b, ln2_g, ln2_b):
    x = x.reshape(N_TOK, D_MODEL)
    xb = x.astype(BF16)
    rw_pad = jnp.pad(router_w.astype(F32), ((0, 0), (0, LANES - N_EXPERTS)))
    rw_hi = rw_pad.astype(BF16)
    rw_lo = (rw_pad - rw_hi.astype(F32)).astype(BF16)
    rw_split = jnp.concatenate([rw_hi, rw_lo], axis=1)
    rb_col = router_bias.astype(F32).reshape(N_EXPERTS, 1)
    k_tables = _rope_tables(1.0)
    q_tables = _rope_tables(1.0 / math.sqrt(HEAD_DIM))
    dils = [d for _, d in BRANCHES]
    k_sh = v_sh = None
    for i in range(DEPTH):
        if i < N_A_LAYERS:
            y = _conv_proj(xb, a_w_in, a_conv_w, i)
            lhs, w_out = [y], a_w_out[i].astype(BF16)
        else:
            j = i - N_A_LAYERS
            if k_sh is None:
                kv3 = kv_w[None]
                k_sh = [_proj(xb, kv3, 0, g, dils[g], k_tables, rope=True) for g in range(N_BRANCH)]
                v_sh = [_proj(xb, kv3, 0, N_BRANCH + g, dils[g], k_tables, rope=False) for g in range(N_BRANCH)]
            q = [_proj(xb, b_w_q, j, g, dils[g], q_tables, rope=True) for g in range(N_BRANCH)]
            outs = [_attn_branch(q[g], k_sh[g], v_sh[g], dils[g]) for g in range(N_BRANCH)]
            lhs, w_out = [o for o, _ in outs] + [l for _, l in outs], b_w_o[j].astype(BF16)
        x1, xp, eidx, gates = _out_ln_route(lhs, w_out, x, ln1_g[i], ln1_b[i], rw_split, rb_col)
        dest0, dest1, last_blk, nblk_e, sched, gates_t = _build_plan(eidx, gates)
        xs = _moe_dispatch(xp, dest0, dest1, last_blk, nblk_e)
        ys = _moe_ffn(xs, sched, moe_w1, moe_w3, moe_w2, i)
        x, xb = _combine_ln(x1, ys, dest0, dest1, gates_t, ln2_g[i], ln2_b[i])
    return x.reshape(BATCH, SEQ, D_MODEL)
```

```python
import functools
import math

import jax
import jax.numpy as jnp
from jax import lax
from jax.experimental import pallas as pl
from jax.experimental.pallas import tpu as pltpu

D_MODEL = 2048
BATCH = 4
SEQ = 4096
DEPTH = 4
N_TOK = BATCH * SEQ
N_A_LAYERS = DEPTH // 2
CONV_WIDTH = 3
BRANCHES = ((128, 1), (512, 4), (2048, 16))
N_BRANCH = len(BRANCHES)
HEADS_PER_BRANCH = 8
HEAD_DIM = 128
BRANCH_WIDTH = HEADS_PER_BRANCH * HEAD_DIM
SPAN = 128
ROT_DIM = HEAD_DIM // 4
ROPE_THETA = 500000.0
N_EXPERTS = 16
N_GROUPS = 4
EXPERTS_PER_GROUP = N_EXPERTS // N_GROUPS
TOP_K = 2
D_FF = 1408
DEEPNORM_ALPHA = (2.0 * DEPTH) ** 0.25
LN_EPS = 1e-5

F32 = jnp.float32
BF16 = jnp.bfloat16
U32 = jnp.uint32

VMEM_LIMIT_BYTES = 56 * 1024 * 1024
LANES = 128
HALF = D_MODEL // 2
PACK_ROWS = HALF // LANES

CONV_TM, CONV_TN = 1024, 256
PROJ_TM = 512
OUT_TM = 256
ATTN_QB = 256
MOE_BLK = 256
MOE_NBLK = (N_TOK * TOP_K) // MOE_BLK + N_EXPERTS
MOE_ROWS = MOE_NBLK * MOE_BLK
MOE_CHUNKS = 8
MOE_UP_ROWS = D_MODEL // MOE_CHUNKS
MOE_DOWN_ROWS = D_FF // MOE_CHUNKS
MOE_STEPS = MOE_NBLK + N_EXPERTS * MOE_CHUNKS
DISPATCH_TM = 512
COMB_TM = 512
DMA_UNROLL = 8


def _params(n_axes):
    return pltpu.CompilerParams(
        dimension_semantics=("arbitrary",) * n_axes,
        vmem_limit_bytes=VMEM_LIMIT_BYTES,
    )


def _pack_pair(lo, hi):
    def rne(v):
        b = lax.bitcast_convert_type(v, U32)
        return (b + U32(0x7FFF) + ((b >> 16) & U32(1))) >> 16
    return rne(lo) | (rne(hi) << 16)


def _unpack_pair(w):
    lo = lax.bitcast_convert_type(w << 16, F32)
    hi = lax.bitcast_convert_type(w & U32(0xFFFF0000), F32)
    return lo, hi


def _store_packed(dst_ref, rows, v):
    packed = _pack_pair(v[:, :HALF], v[:, HALF:])
    for c in range(PACK_ROWS):
        dst_ref[pl.ds(c, rows, stride=PACK_ROWS), :] = packed[:, c * LANES:(c + 1) * LANES]


def _conv_proj_kernel(x_ref, wb_ref, wc_ref, wh_ref, cw_ref, y_ref, wbf_ref, ext_ref):
    i = pl.program_id(1)
    tm = x_ref.shape[0]

    @pl.when(i == 0)
    def _():
        wbf_ref[0] = wb_ref[...].astype(BF16)
        wbf_ref[1] = wc_ref[...].astype(BF16)
        wbf_ref[2] = wh_ref[...].astype(BF16)

    @pl.when(i % (SEQ // tm) == 0)
    def _():
        ext_ref[0:8, :] = jnp.zeros((8, ext_ref.shape[1]), F32)

    cw = cw_ref[...]
    x = x_ref[...]
    gate_b = jnp.dot(x, wbf_ref[0], preferred_element_type=F32)
    gate_c = jnp.dot(x, wbf_ref[1], preferred_element_type=F32)
    h = jnp.dot(x, wbf_ref[2], preferred_element_type=F32)
    u = gate_c * h
    ext_ref[8:8 + tm, :] = u
    u1 = ext_ref[7:7 + tm, :]
    u2 = ext_ref[6:6 + tm, :]
    conv = cw[2:3, :] * u + cw[1:2, :] * u1 + cw[0:1, :] * u2
    y_ref[...] = (gate_b * conv).astype(BF16)
    ext_ref[0:8, :] = ext_ref[tm:tm + 8, :]


def _conv_proj(xb, w_in, conv_w, layer):
    tm, tn = CONV_TM, CONV_TN
    nj = D_MODEL // tn
    wspec = lambda off: pl.BlockSpec((None, D_MODEL, tn), lambda j, i: (layer, 0, j + off * nj))
    return pl.pallas_call(
        _conv_proj_kernel,
        out_shape=jax.ShapeDtypeStruct((N_TOK, D_MODEL), BF16),
        grid=(nj, N_TOK // tm),
        in_specs=[
            pl.BlockSpec((tm, D_MODEL), lambda j, i: (i, 0)),
            wspec(0), wspec(1), wspec(2),
            pl.BlockSpec((None, CONV_WIDTH, tn), lambda j, i: (layer, 0, j)),
        ],
        out_specs=pl.BlockSpec((tm, tn), lambda j, i: (i, j)),
        scratch_shapes=[pltpu.VMEM((3, D_MODEL, tn), BF16), pltpu.VMEM((tm + 8, tn), F32)],
        compiler_params=_params(2),
        name="conv_proj",
    )(xb, w_in, w_in, w_in, conv_w)


def _proj_kernel(x_ref, w_ref, c_ref, s1_ref, s2_ref, o_ref, wbf_ref, rot_ref, *, rope, dilation):
    tm = x_ref.shape[0]

    @pl.when(pl.program_id(0) == 0)
    def _():
        wbf_ref[...] = w_ref[...].astype(BF16)

    acc = jnp.dot(x_ref[...], wbf_ref[...], preferred_element_type=F32)
    if rope:
        c = c_ref[...]
        s1 = s1_ref[...]
        s2 = s2_ref[...]
    for hd in range(HEADS_PER_BRANCH):
        sl = slice(hd * HEAD_DIM, (hd + 1) * HEAD_DIM)
        t = acc[:, sl]
        if rope:
            t = t * c + pltpu.roll(t, HEAD_DIM - ROT_DIM // 2, 1) * s1 + pltpu.roll(t, ROT_DIM // 2, 1) * s2
        if dilation == 1:
            o_ref[:, sl] = t.astype(BF16)
        else:
            rot_ref[hd] = t
            for r in range(dilation):
                piece = rot_ref.at[hd][pl.ds(r, tm // dilation, stride=dilation), :]
                o_ref[:, r * BRANCH_WIDTH + hd * HEAD_DIM:r * BRANCH_WIDTH + (hd + 1) * HEAD_DIM] = piece.astype(BF16)


def _proj(xb, w3d, layer, col, dilation, tables, rope):
    tm = PROJ_TM
    w = BRANCH_WIDTH
    tab = pl.BlockSpec((tm, HEAD_DIM), lambda i: (i % (SEQ // tm), 0))
    return pl.pallas_call(
        functools.partial(_proj_kernel, rope=rope, dilation=dilation),
        out_shape=jax.ShapeDtypeStruct((N_TOK // dilation, dilation * w), BF16),
        grid=(N_TOK // tm,),
        in_specs=[
            pl.BlockSpec((tm, D_MODEL), lambda i: (i, 0)),
            pl.BlockSpec((None, D_MODEL, w), lambda i: (layer, 0, col)),
            tab, tab, tab,
        ],
        out_specs=pl.BlockSpec((tm // dilation, dilation * w), lambda i: (i, 0)),
        scratch_shapes=[pltpu.VMEM((D_MODEL, w), BF16), pltpu.VMEM((HEADS_PER_BRANCH, tm, HEAD_DIM), F32)],
        compiler_params=_params(1),
        name=("proj_rope" if rope else "proj") + f"_d{dilation}",
    )(xb, w3d, *tables)


def _rope_tables(scale):
    inv_freq = ROPE_THETA ** (-jnp.arange(0, ROT_DIM, 2, dtype=F32) / ROT_DIM)
    ang = jnp.arange(SEQ, dtype=F32)[:, None] * inv_freq[None, :]
    cos, sin = jnp.cos(ang), jnp.sin(ang)
    half = ROT_DIM // 2
    rest = HEAD_DIM - ROT_DIM
    c = jnp.concatenate([cos, cos, jnp.ones((SEQ, rest), F32)], axis=1) * scale
    s1 = jnp.concatenate([-sin, jnp.zeros((SEQ, HEAD_DIM - half), F32)], axis=1) * scale
    s2 = jnp.concatenate([jnp.zeros((SEQ, half), F32), sin, jnp.zeros((SEQ, rest), F32)], axis=1) * scale
    return c, s1, s2


def _attn_kernel(q_ref, kc_ref, kp_ref, vc_ref, vp_ref, o_ref, l_ref):
    qb = q_ref.shape[0]
    qi = lax.broadcasted_iota(jnp.int32, (SPAN, 2 * SPAN), 0)
    kj = lax.broadcasted_iota(jnp.int32, (SPAN, 2 * SPAN), 1)
    dist = SPAN + qi - kj
    band = (dist >= 0) & (dist <= SPAN)
    first_key = jnp.where(pl.program_id(1) == 0, SPAN, 0)
    band_first = band & (kj >= first_key)
    for a in range(qb // SPAN):
        rows = slice(a * SPAN, (a + 1) * SPAN)
        for hd in range(HEADS_PER_BRANCH):
            cols = slice(hd * HEAD_DIM, (hd + 1) * HEAD_DIM)
            q = q_ref[rows, cols]
            if a == 0:
                k_prev, v_prev = kp_ref[:, cols], vp_ref[:, cols]
                mask = band_first
            else:
                prows = slice((a - 1) * SPAN, a * SPAN)
                k_prev, v_prev = kc_ref[prows, cols], vc_ref[prows, cols]
                mask = band
            kk = jnp.concatenate([k_prev, kc_ref[rows, cols]], axis=0)
            vv = jnp.concatenate([v_prev, vc_ref[rows, cols]], axis=0)
            s = lax.dot_general(q, kk, (((1,), (1,)), ((), ())), preferred_element_type=F32)
            s = jnp.where(mask, s, -jnp.inf)
            m = jnp.max(s, axis=-1, keepdims=True)
            p = jnp.exp(s - m)
            den = jnp.sum(p, axis=-1, keepdims=True)
            o = jnp.dot(p.astype(BF16), vv, preferred_element_type=F32) / den
            o_ref[rows, cols] = o.astype(BF16)
            l_ref[rows, cols] = jnp.broadcast_to(m + jnp.log(den), (SPAN, HEAD_DIM))


def _attn_branch(q, k, v, dilation):
    sub = SEQ // dilation
    qb = min(ATTN_QB, sub)
    w = BRANCH_WIDTH
    view = lambda t: t.reshape(BATCH, sub, dilation * w)
    cur = pl.BlockSpec((None, qb, w), lambda b, n, r: (b, n, r))
    prev = pl.BlockSpec((None, SPAN, w),
                        lambda b, n, r: (b, jnp.maximum(n * (qb // SPAN) - 1, 0), r))
    o, l = pl.pallas_call(
        _attn_kernel,
        out_shape=(jax.ShapeDtypeStruct((BATCH, sub, dilation * w), BF16),
                   jax.ShapeDtypeStruct((BATCH, sub, dilation * w), F32)),
        grid=(BATCH, sub // qb, dilation),
        in_specs=[cur, cur, prev, cur, prev],
        out_specs=(cur, cur),
        compiler_params=_params(3),
        name=f"dilated_attn_d{dilation}",
    )(view(q), view(k), view(k), view(v), view(v))
    return o.reshape(N_TOK // dilation, dilation * w), l.reshape(N_TOK // dilation, dilation * w)


def _layer_norm(z, g, b):
    mean = jnp.mean(z, axis=-1, keepdims=True)
    zc = z - mean
    var = jnp.mean(zc * zc, axis=-1, keepdims=True)
    return zc * lax.rsqrt(var + LN_EPS) * g + b


def _route(logits_t, bias_col):
    scores = jax.nn.sigmoid(logits_t)
    sel = scores + bias_col
    sel_r = [sel[e:e + 1, :] for e in range(N_EXPERTS)]
    sc_r = [scores[e:e + 1, :] for e in range(N_EXPERTS)]
    best_g = None
    for g in range(N_GROUPS):
        v = sel_r[g * EXPERTS_PER_GROUP:(g + 1) * EXPERTS_PER_GROUP]
        gs = None
        for a in range(EXPERTS_PER_GROUP):
            for b in range(a + 1, EXPERTS_PER_GROUP):
                ps = v[a] + v[b]
                gs = ps if gs is None else jnp.maximum(gs, ps)
        if best_g is None:
            best_g, gidx = gs, jnp.zeros_like(gs, dtype=jnp.int32)
        else:
            upd = gs > best_g
            best_g = jnp.where(upd, gs, best_g)
            gidx = jnp.where(upd, g, gidx)
    cand, raw = [], []
    for j in range(EXPERTS_PER_GROUP):
        cv, rv = sel_r[j], sc_r[j]
        for g in range(1, N_GROUPS):
            pick = gidx == g
            cv = jnp.where(pick, sel_r[g * EXPERTS_PER_GROUP + j], cv)
            rv = jnp.where(pick, sc_r[g * EXPERTS_PER_GROUP + j], rv)
        cand.append(cv)
        raw.append(rv)

    def argmax4(vals):
        best, idx = vals[0], jnp.zeros_like(gidx)
        for j in range(1, EXPERTS_PER_GROUP):
            upd = vals[j] > best
            best = jnp.where(upd, vals[j], best)
            idx = jnp.where(upd, j, idx)
        return idx

    i1 = argmax4(cand)
    i2 = argmax4([jnp.where(i1 == j, -jnp.inf, cand[j]) for j in range(EXPERTS_PER_GROUP)])
    pick_raw = lambda idx: sum(jnp.where(idx == j, raw[j], 0.0) for j in range(EXPERTS_PER_GROUP))
    g1, g2 = pick_raw(i1), pick_raw(i2)
    tot = g1 + g2
    e1 = gidx * EXPERTS_PER_GROUP + i1
    e2 = gidx * EXPERTS_PER_GROUP + i2
    return e1, e2, g1 / tot, g2 / tot


def _out_ln_route_tail(mix, x_ref, g_ref, b_ref, rw_ref, rb_ref, x1_ref, xp_ref, e_ref, gt_ref):
    tm = x_ref.shape[0]
    z = DEEPNORM_ALPHA * x_ref[...] + mix
    x1 = _layer_norm(z, g_ref[...], b_ref[...])
    x1_ref[...] = x1
    _store_packed(xp_ref, tm, x1)
    x_hi = x1.astype(BF16)
    x_lo = (x1 - x_hi.astype(F32)).astype(BF16)
    p_hi = jnp.dot(x_hi, rw_ref[...], preferred_element_type=F32)
    p_lo = jnp.dot(x_lo, rw_ref[:, :LANES], preferred_element_type=F32)
    logits = p_hi[:, :LANES] + (p_hi[:, LANES:] + p_lo)
    logits_t = logits.T[0:N_EXPERTS, :]
    e1, e2, g1, g2 = _route(logits_t, rb_ref[...])
    e_ref[0:1, :] = e1
    e_ref[1:2, :] = e2
    gt_ref[0:1, :] = g1
    gt_ref[1:2, :] = g2


def _out_ln_route_a_kernel(a_ref, w_ref, *rest):
    mix = jnp.dot(a_ref[...], w_ref[...], preferred_element_type=F32)
    _out_ln_route_tail(mix, *rest)


def _out_ln_route_b_kernel(o0_ref, o1_ref, o2_ref, l0_ref, l1_ref, l2_ref, w_ref,
                           x_ref, g_ref, b_ref, rw_ref, rb_ref, x1_ref, xp_ref, e_ref, gt_ref,
                           os_ref, ls_ref):
    tm = x_ref.shape[0]
    merged = []
    for hd in range(HEADS_PER_BRANCH):
        for g, (o_ref, l_ref) in enumerate(((o1_ref, l1_ref), (o2_ref, l2_ref))):
            d = BRANCHES[g + 1][1]
            for r in range(d):
                cols = slice(r * BRANCH_WIDTH + hd * HEAD_DIM, r * BRANCH_WIDTH + (hd + 1) * HEAD_DIM)
                os_ref.at[g, hd][pl.ds(r, tm // d, stride=d), :] = o_ref[:, cols].astype(F32)
                ls_ref.at[g, hd][pl.ds(r, tm // d, stride=d), :] = l_ref[:, cols]
        cols = slice(hd * HEAD_DIM, (hd + 1) * HEAD_DIM)
        l0, l1, l2 = l0_ref[:, cols], ls_ref[0, hd], ls_ref[1, hd]
        lm = jnp.maximum(jnp.maximum(l0, l1), l2)
        w0, w1, w2 = jnp.exp(l0 - lm), jnp.exp(l1 - lm), jnp.exp(l2 - lm)
        m = (o0_ref[:, cols].astype(F32) * w0 + os_ref[0, hd] * w1 + os_ref[1, hd] * w2) / (w0 + w1 + w2)
        merged.append(m.astype(BF16))
    mix = jnp.dot(jnp.concatenate(merged, axis=1), w_ref[...], preferred_element_type=F32)
    _out_ln_route_tail(mix, x_ref, g_ref, b_ref, rw_ref, rb_ref, x1_ref, xp_ref, e_ref, gt_ref)


def _out_ln_route(lhs, w_bf, x, ln_g, ln_b, rw_split, rb_col):
    tm = OUT_TM
    kdim = w_bf.shape[0]
    row = lambda width: pl.BlockSpec((tm, width), lambda i: (i, 0))
    full = lambda shape: pl.BlockSpec(shape, lambda i: (0,) * len(shape))
    lane_row = pl.BlockSpec((TOP_K, tm), lambda i: (0, i))
    if len(lhs) == 1:
        body, name, lhs_specs, scratch = _out_ln_route_a_kernel, "out_ln_route_conv", [row(kdim)], []
    else:
        body, name = _out_ln_route_b_kernel, "out_ln_route_attn"
        blocked = [pl.BlockSpec((tm // d, d * BRANCH_WIDTH), lambda i: (i, 0)) for _, d in BRANCHES]
        lhs_specs = blocked + blocked
        scratch = [pltpu.VMEM((N_BRANCH - 1, HEADS_PER_BRANCH, tm, HEAD_DIM), F32)] * 2
    return pl.pallas_call(
        body,
        out_shape=(jax.ShapeDtypeStruct((N_TOK, D_MODEL), F32),
                   jax.ShapeDtypeStruct((N_TOK * PACK_ROWS, LANES), U32),
                   jax.ShapeDtypeStruct((TOP_K, N_TOK), jnp.int32),
                   jax.ShapeDtypeStruct((TOP_K, N_TOK), F32)),
        grid=(N_TOK // tm,),
        in_specs=lhs_specs + [
            full((kdim, D_MODEL)), row(D_MODEL), full((1, D_MODEL)), full((1, D_MODEL)),
            full((D_MODEL, 2 * LANES)), full((N_EXPERTS, 1)),
        ],
        out_specs=(row(D_MODEL), pl.BlockSpec((tm * PACK_ROWS, LANES), lambda i: (i, 0)), lane_row, lane_row),
        scratch_shapes=scratch,
        compiler_params=_params(1),
        name=name,
    )(*lhs, w_bf, x, ln_g.reshape(1, D_MODEL), ln_b.reshape(1, D_MODEL), rw_split, rb_col)


def _build_plan(eidx, gates):
    flat_e = eidx.reshape(-1)
    onehot = (flat_e[:, None] == jnp.arange(N_EXPERTS, dtype=jnp.int32)[None, :]).astype(jnp.int32)
    csum = jnp.cumsum(onehot, axis=0)
    rank = jnp.sum(onehot * csum, axis=1) - 1
    counts = csum[-1]
    nblk_e = ((counts + MOE_BLK - 1) // MOE_BLK).astype(jnp.int32)
    blk_end = jnp.cumsum(nblk_e).astype(jnp.int32)
    blk_start = blk_end - nblk_e
    dest = (blk_start[flat_e] * MOE_BLK + rank).astype(jnp.int32)
    n_valid = blk_end[-1]

    chunks = jnp.full((1,), MOE_CHUNKS, jnp.int32)
    phase_len = jnp.concatenate([chunks, jnp.maximum(nblk_e[:-1], MOE_CHUNKS), nblk_e[-1:]])
    phase_end = jnp.cumsum(phase_len).astype(jnp.int32)
    phase_start = phase_end - phase_len
    step = jnp.arange(MOE_STEPS, dtype=jnp.int32)
    ph = jnp.clip(jnp.searchsorted(phase_end, step, side="right"), 0, N_EXPERTS).astype(jnp.int32)
    off = step - phase_start[ph]
    live = step < phase_end[-1]
    ce = jnp.clip(ph - 1, 0, N_EXPERTS - 1)
    nb_c = jnp.where(ph >= 1, nblk_e[ce], 0)
    comp_on = (live & (ph >= 1) & (off < nb_c)).astype(jnp.int32)
    comp_blk = jnp.clip(jnp.where(ph >= 1, blk_start[ce], 0) + jnp.minimum(off, nb_c - 1), 0, n_valid - 1)
    load_on = (live & (ph < N_EXPERTS) & (off < MOE_CHUNKS)).astype(jnp.int32)
    load_e = jnp.minimum(ph, N_EXPERTS - 1)
    load_c = jnp.where(ph < N_EXPERTS, jnp.minimum(off, MOE_CHUNKS - 1), MOE_CHUNKS - 1)
    sched = (comp_on, comp_blk.astype(jnp.int32), (ph + 1) % 2, load_on, load_e, load_c.astype(jnp.int32), ph % 2)
    return dest[:N_TOK], dest[N_TOK:], blk_end - 1, nblk_e, sched, gates.T


def _dispatch_kernel(d0_ref, d1_ref, last_ref, nblk_ref, xp_ref, xs_hbm, zero_ref, zsem, sem):
    i = pl.program_id(0)
    tm = xp_ref.shape[0] // PACK_ROWS
    blk_rows = MOE_BLK * PACK_ROWS

    @pl.when(i == 0)
    def _():
        zero_ref[...] = jnp.zeros_like(zero_ref)

        def zero_copy(e):
            start = pl.multiple_of(last_ref[e] * blk_rows, blk_rows)
            return pltpu.make_async_copy(zero_ref, xs_hbm.at[pl.ds(start, blk_rows)], zsem)

        for e in range(N_EXPERTS):
            @pl.when(nblk_ref[e] > 0)
            def _():
                zero_copy(e).start()
        for e in range(N_EXPERTS):
            @pl.when(nblk_ref[e] > 0)
            def _():
                zero_copy(e).wait()

    def body(r, carry):
        t = i * tm + r
        src = xp_ref.at[pl.ds(pl.multiple_of(r * PACK_ROWS, PACK_ROWS), PACK_ROWS)]
        for k, d_ref in enumerate((d0_ref, d1_ref)):
            dst = xs_hbm.at[pl.ds(pl.multiple_of(d_ref[t] * PACK_ROWS, PACK_ROWS), PACK_ROWS)]
            pltpu.make_async_copy(src, dst, sem).start(priority=k)
        return carry

    lax.fori_loop(0, tm, body, 0, unroll=DMA_UNROLL)
    for _ in range(TOP_K):
        pltpu.make_async_copy(xp_ref, xs_hbm.at[pl.ds(0, tm * PACK_ROWS)], sem).wait()


def _moe_dispatch(xp, dest0, dest1, last_blk, nblk_e):
    tm = DISPATCH_TM
    return pl.pallas_call(
        _dispatch_kernel,
        out_shape=jax.ShapeDtypeStruct((MOE_ROWS * PACK_ROWS, LANES), U32),
        grid_spec=pltpu.PrefetchScalarGridSpec(
            num_scalar_prefetch=4,
            grid=(N_TOK // tm,),
            in_specs=[pl.BlockSpec((tm * PACK_ROWS, LANES), lambda i, *_: (i, 0))],
            out_specs=pl.BlockSpec(memory_space=pl.ANY),
            scratch_shapes=[pltpu.VMEM((MOE_BLK * PACK_ROWS, LANES), U32),
                            pltpu.SemaphoreType.DMA, pltpu.SemaphoreType.DMA],
        ),
        compiler_params=_params(1),
        name="moe_dispatch",
    )(dest0, dest1, last_blk, nblk_e, xp)


def _moe_ffn_kernel(con_ref, cblk_ref, cslot_ref, lon_ref, le_ref, lc_ref, lslot_ref,
                    xs_ref, w1_ref, w3_ref, w2_ref, ys_ref, w13_ref, w2b_ref):
    s = pl.program_id(0)
    blk = MOE_BLK
    n_ff = D_FF // LANES

    @pl.when(lon_ref[s] == 1)
    def _():
        slot = lslot_ref[s]
        rows = pl.ds(pl.multiple_of(lc_ref[s] * MOE_UP_ROWS, MOE_UP_ROWS), MOE_UP_ROWS)
        for j in range(n_ff):
            src = slice(j * LANES, (j + 1) * LANES)
            w13_ref[slot, rows, 2 * j * LANES:(2 * j + 1) * LANES] = w1_ref[:, src].astype(BF16)
            w13_ref[slot, rows, (2 * j + 1) * LANES:(2 * j + 2) * LANES] = w3_ref[:, src].astype(BF16)
        rows2 = pl.ds(pl.multiple_of(lc_ref[s] * MOE_DOWN_ROWS, 16), MOE_DOWN_ROWS)
        w2b_ref[slot, rows2, :] = w2_ref[...].astype(BF16)

    @pl.when(con_ref[s] == 1)
    def _():
        slot = cslot_ref[s]
        pieces = [_unpack_pair(xs_ref[pl.ds(c, blk, stride=PACK_ROWS), :]) for c in range(PACK_ROWS)]
        x = jnp.concatenate([lo.astype(BF16) for lo, _ in pieces] + [hi.astype(BF16) for _, hi in pieces], axis=1)
        g = []
        for q in range(0, n_ff, 2):
            npair = min(2, n_ff - q)
            h = jnp.dot(x, w13_ref[slot, :, 2 * q * LANES:2 * (q + npair) * LANES], preferred_element_type=F32)
            for k in range(npair):
                h1, h3 = h[:, 2 * k * LANES:(2 * k + 1) * LANES], h[:, (2 * k + 1) * LANES:(2 * k + 2) * LANES]
                g.append((jax.nn.silu(h1) * h3).astype(BF16))
        y = jnp.dot(jnp.concatenate(g, axis=1), w2b_ref[slot], preferred_element_type=F32)
        _store_packed(ys_ref, blk, y)


def _moe_ffn(xs, sched, w1, w3, w2, layer):
    blk = MOE_BLK
    rows = pl.BlockSpec((blk * PACK_ROWS, LANES), lambda s, con, cblk, *_: (cblk[s], 0))
    up = pl.BlockSpec((None, None, MOE_UP_ROWS, D_FF),
                      lambda s, con, cblk, cslot, lon, le, lc, *_: (layer, le[s], lc[s], 0))
    down = pl.BlockSpec((None, None, MOE_DOWN_ROWS, D_MODEL),
                        lambda s, con, cblk, cslot, lon, le, lc, *_: (layer, le[s], lc[s], 0))
    return pl.pallas_call(
        _moe_ffn_kernel,
        out_shape=jax.ShapeDtypeStruct((MOE_ROWS * PACK_ROWS, LANES), U32),
        grid_spec=pltpu.PrefetchScalarGridSpec(
            num_scalar_prefetch=len(sched),
            grid=(MOE_STEPS,),
            in_specs=[rows, up, up, down],
            out_specs=rows,
            scratch_shapes=[pltpu.VMEM((2, D_MODEL, 2 * D_FF), BF16), pltpu.VMEM((2, D_FF, D_MODEL), BF16)],
        ),
        compiler_params=_params(1),
        name="moe_ffn",
    )(*sched, xs, w1, w3, w2)


def _combine_ln_kernel(d0_ref, d1_ref, x1_ref, gt_ref, ys_hbm, g_ref, b_ref, x2_ref, x2b_ref,
                       buf_ref, ffn_ref, sems):
    i = pl.program_id(0)
    tm = x1_ref.shape[0]

    def issue(tile):
        slot = tile % 2

        def body(r, carry):
            t = tile * tm + r
            for k, d_ref in enumerate((d0_ref, d1_ref)):
                src = ys_hbm.at[pl.ds(pl.multiple_of(d_ref[t] * PACK_ROWS, PACK_ROWS), PACK_ROWS)]
                dst = buf_ref.at[slot, k, pl.ds(pl.multiple_of(r * PACK_ROWS, PACK_ROWS), PACK_ROWS)]
                pltpu.make_async_copy(src, dst, sems.at[slot]).start(priority=k)
            return carry

        lax.fori_loop(0, tm, body, 0, unroll=DMA_UNROLL)

    @pl.when(i == 0)
    def _():
        issue(0)

    @pl.when(i + 1 < pl.num_programs(0))
    def _():
        issue(i + 1)

    slot = i % 2
    for k in range(TOP_K):
        pltpu.make_async_copy(ys_hbm.at[pl.ds(0, tm * PACK_ROWS)], buf_ref.at[slot, k], sems.at[slot]).wait()
    gt = gt_ref[...]
    g0, g1 = gt[:, 0:1], gt[:, 1:2]
    for c in range(PACK_ROWS):
        lo0, hi0 = _unpack_pair(buf_ref.at[slot, 0][pl.ds(c, tm, stride=PACK_ROWS), :])
        lo1, hi1 = _unpack_pair(buf_ref.at[slot, 1][pl.ds(c, tm, stride=PACK_ROWS), :])
        ffn_ref[:, c * LANES:(c + 1) * LANES] = lo0 * g0 + lo1 * g1
        ffn_ref[:, HALF + c * LANES:HALF + (c + 1) * LANES] = hi0 * g0 + hi1 * g1
    x2 = _layer_norm(DEEPNORM_ALPHA * x1_ref[...] + ffn_ref[...], g_ref[...], b_ref[...])
    x2_ref[...] = x2
    x2b_ref[...] = x2.astype(BF16)


def _combine_ln(x1, ys, dest0, dest1, gates_t, ln_g, ln_b):
    tm = COMB_TM
    row = lambda width: pl.BlockSpec((tm, width), lambda i, d0, d1: (i, 0))
    vec = pl.BlockSpec((1, D_MODEL), lambda i, d0, d1: (0, 0))
    return pl.pallas_call(
        _combine_ln_kernel,
        out_shape=(jax.ShapeDtypeStruct((N_TOK, D_MODEL), F32),
                   jax.ShapeDtypeStruct((N_TOK, D_MODEL), BF16)),
        grid_spec=pltpu.PrefetchScalarGridSpec(
            num_scalar_prefetch=2,
            grid=(N_TOK // tm,),
            in_specs=[row(D_MODEL), row(TOP_K), pl.BlockSpec(memory_space=pl.ANY), vec, vec],
            out_specs=(row(D_MODEL), row(D_MODEL)),
            scratch_shapes=[pltpu.VMEM((2, TOP_K, tm * PACK_ROWS, LANES), U32),
                            pltpu.VMEM((tm, D_MODEL), F32),
                            pltpu.SemaphoreType.DMA((2,))],
        ),
        compiler_params=_params(1),
        name="combine_ln",
    )(dest0, dest1, x1, gates_t, ys, ln_g.reshape(1, D_MODEL), ln_b.reshape(1, D_MODEL))


def kernel(x, a_w_in, a_conv_w, a_w_out, kv_w, b_w_q, b_w_o, router_w, router_bias,
           moe_w1, moe_w3, moe_w2, ln1_g, ln1_b, ln2_g, ln2_b):
    x = x.reshape(N_TOK, D_MODEL)
    xb = x.astype(BF16)
    rw_pad = jnp.pad(router_w.astype(F32), ((0, 0), (0, LANES - N_EXPERTS)))
    rw_hi = rw_pad.astype(BF16)
    rw_lo = (rw_pad - rw_hi.astype(F32)).astype(BF16)
    rw_split = jnp.concatenate([rw_hi, rw_lo], axis=1)
    rb_col = router_bias.astype(F32).reshape(N_EXPERTS, 1)
    k_tables = _rope_tables(1.0)
    q_tables = _rope_tables(1.0 / math.sqrt(HEAD_DIM))
    dils = [d for _, d in BRANCHES]
    k_sh = v_sh = None
    for i in range(DEPTH):
        if i < N_A_LAYERS:
            y = _conv_proj(xb, a_w_in, a_conv_w, i)
            lhs, w_out = [y], a_w_out[i].astype(BF16)
        else:
            j = i - N_A_LAYERS
            if k_sh is None:
                kv3 = kv_w[None]
                k_sh = [_proj(xb, kv3, 0, g, dils[g], k_tables, rope=True) for g in range(N_BRANCH)]
                v_sh = [_proj(xb, kv3, 0, N_BRANCH + g, dils[g], k_tables, rope=False) for g in range(N_BRANCH)]
            q = [_proj(xb, b_w_q, j, g, dils[g], q_tables, rope=True) for g in range(N_BRANCH)]
            outs = [_attn_branch(q[g], k_sh[g], v_sh[g], dils[g]) for g in range(N_BRANCH)]
            lhs, w_out = [o for o, _ in outs] + [l for _, l in outs], b_w_o[j].astype(BF16)
        x1, xp, eidx, gates = _out_ln_route(lhs, w_out, x, ln1_g[i], ln1_b[i], rw_split, rb_col)
        dest0, dest1, last_blk, nblk_e, sched, gates_t = _build_plan(eidx, gates)
        xs = _moe_dispatch(xp, dest0, dest1, last_blk, nblk_e)
        ys = _moe_ffn(xs, sched, moe_w1, moe_w3, moe_w2, i)
        x, xb = _combine_ln(x1, ys, dest0, dest1, gates_t, ln2_g[i], ln2_b[i])
    return x.reshape(BATCH, SEQ, D_MODEL)
```

```python
import functools
import math

import jax
import jax.numpy as jnp
from jax import lax
from jax.experimental import pallas as pl
from jax.experimental.pallas import tpu as pltpu

D_MODEL = 2048
BATCH = 4
SEQ = 4096
DEPTH = 4
N_TOK = BATCH * SEQ
N_A_LAYERS = DEPTH // 2
CONV_WIDTH = 3
BRANCHES = ((128, 1), (512, 4), (2048, 16))
N_BRANCH = len(BRANCHES)
HEADS_PER_BRANCH = 8
HEAD_DIM = 128
BRANCH_WIDTH = HEADS_PER_BRANCH * HEAD_DIM
SPAN = 128
ROT_DIM = HEAD_DIM // 4
ROPE_THETA = 500000.0
N_EXPERTS = 16
N_GROUPS = 4
EXPERTS_PER_GROUP = N_EXPERTS // N_GROUPS
TOP_K = 2
D_FF = 1408
DEEPNORM_ALPHA = (2.0 * DEPTH) ** 0.25
LN_EPS = 1e-5

F32 = jnp.float32
BF16 = jnp.bfloat16

VMEM_LIMIT_BYTES = 56 * 1024 * 1024
LANES = 128
PACK_ROWS = D_MODEL // LANES

CONV_TM, CONV_TN = 2048, 256
PROJ_TM = 1024
OUT_TM = 256
ATTN_QB = 512
MOE_BLK = 256
MOE_NBLK = (N_TOK * TOP_K) // MOE_BLK + N_EXPERTS
MOE_ROWS = MOE_NBLK * MOE_BLK
MOE_CHUNKS = 8
MOE_UP_ROWS = D_MODEL // MOE_CHUNKS
MOE_DOWN_ROWS = D_FF // MOE_CHUNKS
MOE_STEPS = MOE_NBLK + N_EXPERTS * MOE_CHUNKS
DISPATCH_TM = 512
COMB_TM = 512
DMA_UNROLL = 8


def _params(n_axes):
    return pltpu.CompilerParams(
        dimension_semantics=("arbitrary",) * n_axes,
        vmem_limit_bytes=VMEM_LIMIT_BYTES,
    )


def _store_token_rows(dst_ref, rows, v):
    for c in range(PACK_ROWS):
        dst_ref[pl.ds(c, rows, stride=PACK_ROWS), :] = v[:, c * LANES:(c + 1) * LANES]


def _load_token_rows(src_ref, rows):
    return [src_ref[pl.ds(c, rows, stride=PACK_ROWS), :] for c in range(PACK_ROWS)]


def _conv_proj_kernel(x_ref, wb_ref, wc_ref, wh_ref, cw_ref, y_ref, wbf_ref, ext_ref):
    i = pl.program_id(1)
    tm = x_ref.shape[0]

    @pl.when(i == 0)
    def _():
        wbf_ref[0] = wb_ref[...].astype(BF16)
        wbf_ref[1] = wc_ref[...].astype(BF16)
        wbf_ref[2] = wh_ref[...].astype(BF16)

    @pl.when(i % (SEQ // tm) == 0)
    def _():
        ext_ref[0:8, :] = jnp.zeros((8, ext_ref.shape[1]), F32)

    cw = cw_ref[...]
    x = x_ref[...]
    gate_b = jnp.dot(x, wbf_ref[0], preferred_element_type=F32)
    gate_c = jnp.dot(x, wbf_ref[1], preferred_element_type=F32)
    h = jnp.dot(x, wbf_ref[2], preferred_element_type=F32)
    u = gate_c * h
    ext_ref[8:8 + tm, :] = u
    u1 = ext_ref[7:7 + tm, :]
    u2 = ext_ref[6:6 + tm, :]
    conv = cw[2:3, :] * u + cw[1:2, :] * u1 + cw[0:1, :] * u2
    y_ref[...] = (gate_b * conv).astype(BF16)
    ext_ref[0:8, :] = ext_ref[tm:tm + 8, :]


def _conv_proj(xb, w_in, conv_w, layer):
    tm, tn = CONV_TM, CONV_TN
    nj = D_MODEL // tn
    wspec = lambda off: pl.BlockSpec((None, D_MODEL, tn), lambda j, i: (layer, 0, j + off * nj))
    return pl.pallas_call(
        _conv_proj_kernel,
        out_shape=jax.ShapeDtypeStruct((N_TOK, D_MODEL), BF16),
        grid=(nj, N_TOK // tm),
        in_specs=[
            pl.BlockSpec((tm, D_MODEL), lambda j, i: (i, 0)),
            wspec(0), wspec(1), wspec(2),
            pl.BlockSpec((None, CONV_WIDTH, tn), lambda j, i: (layer, 0, j)),
        ],
        out_specs=pl.BlockSpec((tm, tn), lambda j, i: (i, j)),
        scratch_shapes=[pltpu.VMEM((3, D_MODEL, tn), BF16), pltpu.VMEM((tm + 8, tn), F32)],
        compiler_params=_params(2),
        name="conv_proj",
    )(xb, w_in, w_in, w_in, conv_w)


def _proj_kernel(x_ref, w_ref, c_ref, s1_ref, s2_ref, o_ref, wbf_ref, rot_ref, *, rope, dilation):
    tm = x_ref.shape[0]

    @pl.when(pl.program_id(0) == 0)
    def _():
        wbf_ref[...] = w_ref[...].astype(BF16)

    acc = jnp.dot(x_ref[...], wbf_ref[...], preferred_element_type=F32)
    if rope:
        c = c_ref[...]
        s1 = s1_ref[...]
        s2 = s2_ref[...]
    for hd in range(HEADS_PER_BRANCH):
        sl = slice(hd * HEAD_DIM, (hd + 1) * HEAD_DIM)
        t = acc[:, sl]
        if rope:
            t = t * c + pltpu.roll(t, HEAD_DIM - ROT_DIM // 2, 1) * s1 + pltpu.roll(t, ROT_DIM // 2, 1) * s2
        if dilation == 1:
            o_ref[:, sl] = t.astype(BF16)
        else:
            rot_ref[hd] = t
            for r in range(dilation):
                piece = rot_ref.at[hd][pl.ds(r, tm // dilation, stride=dilation), :]
                o_ref[:, r * BRANCH_WIDTH + hd * HEAD_DIM:r * BRANCH_WIDTH + (hd + 1) * HEAD_DIM] = piece.astype(BF16)


def _proj(xb, w3d, layer, col, dilation, tables, rope):
    tm = PROJ_TM
    w = BRANCH_WIDTH
    tab = pl.BlockSpec((tm, HEAD_DIM), lambda i: (i % (SEQ // tm), 0))
    return pl.pallas_call(
        functools.partial(_proj_kernel, rope=rope, dilation=dilation),
        out_shape=jax.ShapeDtypeStruct((N_TOK // dilation, dilation * w), BF16),
        grid=(N_TOK // tm,),
        in_specs=[
            pl.BlockSpec((tm, D_MODEL), lambda i: (i, 0)),
            pl.BlockSpec((None, D_MODEL, w), lambda i: (layer, 0, col)),
            tab, tab, tab,
        ],
        out_specs=pl.BlockSpec((tm // dilation, dilation * w), lambda i: (i, 0)),
        scratch_shapes=[pltpu.VMEM((D_MODEL, w), BF16), pltpu.VMEM((HEADS_PER_BRANCH, tm, HEAD_DIM), F32)],
        compiler_params=_params(1),
        name=("proj_rope" if rope else "proj") + f"_d{dilation}",
    )(xb, w3d, *tables)


def _rope_tables(scale):
    inv_freq = ROPE_THETA ** (-jnp.arange(0, ROT_DIM, 2, dtype=F32) / ROT_DIM)
    ang = jnp.arange(SEQ, dtype=F32)[:, None] * inv_freq[None, :]
    cos, sin = jnp.cos(ang), jnp.sin(ang)
    half = ROT_DIM // 2
    rest = HEAD_DIM - ROT_DIM
    c = jnp.concatenate([cos, cos, jnp.ones((SEQ, rest), F32)], axis=1) * scale
    s1 = jnp.concatenate([-sin, jnp.zeros((SEQ, HEAD_DIM - half), F32)], axis=1) * scale
    s2 = jnp.concatenate([jnp.zeros((SEQ, half), F32), sin, jnp.zeros((SEQ, rest), F32)], axis=1) * scale
    return c, s1, s2


def _attn_kernel(q_ref, kc_ref, kp_ref, vc_ref, vp_ref, o_ref, l_ref):
    qb = q_ref.shape[0]
    qi = lax.broadcasted_iota(jnp.int32, (SPAN, 2 * SPAN), 0)
    kj = lax.broadcasted_iota(jnp.int32, (SPAN, 2 * SPAN), 1)
    dist = SPAN + qi - kj
    band = (dist >= 0) & (dist <= SPAN)
    first_key = jnp.where(pl.program_id(1) == 0, SPAN, 0)
    band_first = band & (kj >= first_key)
    for a in range(qb // SPAN):
        rows = slice(a * SPAN, (a + 1) * SPAN)
        for hd in range(HEADS_PER_BRANCH):
            cols = slice(hd * HEAD_DIM, (hd + 1) * HEAD_DIM)
            q = q_ref[rows, cols]
            if a == 0:
                k_prev, v_prev = kp_ref[:, cols], vp_ref[:, cols]
                mask = band_first
            else:
                prows = slice((a - 1) * SPAN, a * SPAN)
                k_prev, v_prev = kc_ref[prows, cols], vc_ref[prows, cols]
                mask = band
            kk = jnp.concatenate([k_prev, kc_ref[rows, cols]], axis=0)
            vv = jnp.concatenate([v_prev, vc_ref[rows, cols]], axis=0)
            s = lax.dot_general(q, kk, (((1,), (1,)), ((), ())), preferred_element_type=F32)
            s = jnp.where(mask, s, -jnp.inf)
            m = jnp.max(s, axis=-1, keepdims=True)
            p = jnp.exp(s - m)
            den = jnp.sum(p, axis=-1, keepdims=True)
            o = jnp.dot(p.astype(BF16), vv, preferred_element_type=F32) / den
            o_ref[rows, cols] = o.astype(BF16)
            l_ref[rows, cols] = jnp.broadcast_to(m + jnp.log(den), (SPAN, HEAD_DIM))


def _attn_branch(q, k, v, dilation):
    sub = SEQ // dilation
    qb = min(ATTN_QB, sub)
    w = BRANCH_WIDTH
    view = lambda t: t.reshape(BATCH, sub, dilation * w)
    cur = pl.BlockSpec((None, qb, w), lambda b, n, r: (b, n, r))
    prev = pl.BlockSpec((None, SPAN, w),
                        lambda b, n, r: (b, jnp.maximum(n * (qb // SPAN) - 1, 0), r))
    o, l = pl.pallas_call(
        _attn_kernel,
        out_shape=(jax.ShapeDtypeStruct((BATCH, sub, dilation * w), BF16),
                   jax.ShapeDtypeStruct((BATCH, sub, dilation * w), F32)),
        grid=(BATCH, sub // qb, dilation),
        in_specs=[cur, cur, prev, cur, prev],
        out_specs=(cur, cur),
        compiler_params=_params(3),
        name=f"dilated_attn_d{dilation}",
    )(view(q), view(k), view(k), view(v), view(v))
    return o.reshape(N_TOK // dilation, dilation * w), l.reshape(N_TOK // dilation, dilation * w)


def _layer_norm(z, g, b):
    mean = jnp.mean(z, axis=-1, keepdims=True)
    zc = z - mean
    var = jnp.mean(zc * zc, axis=-1, keepdims=True)
    return zc * lax.rsqrt(var + LN_EPS) * g + b


def _route(logits_t, bias_col):
    scores = jax.nn.sigmoid(logits_t)
    sel = scores + bias_col
    sel_r = [sel[e:e + 1, :] for e in range(N_EXPERTS)]
    sc_r = [scores[e:e + 1, :] for e in range(N_EXPERTS)]
    best_g = None
    for g in range(N_GROUPS):
        v = sel_r[g * EXPERTS_PER_GROUP:(g + 1) * EXPERTS_PER_GROUP]
        gs = None
        for a in range(EXPERTS_PER_GROUP):
            for b in range(a + 1, EXPERTS_PER_GROUP):
                ps = v[a] + v[b]
                gs = ps if gs is None else jnp.maximum(gs, ps)
        if best_g is None:
            best_g, gidx = gs, jnp.zeros_like(gs, dtype=jnp.int32)
        else:
            upd = gs > best_g
            best_g = jnp.where(upd, gs, best_g)
            gidx = jnp.where(upd, g, gidx)
    cand, raw = [], []
    for j in range(EXPERTS_PER_GROUP):
        cv, rv = sel_r[j], sc_r[j]
        for g in range(1, N_GROUPS):
            pick = gidx == g
            cv = jnp.where(pick, sel_r[g * EXPERTS_PER_GROUP + j], cv)
            rv = jnp.where(pick, sc_r[g * EXPERTS_PER_GROUP + j], rv)
        cand.append(cv)
        raw.append(rv)

    def argmax4(vals):
        best, idx = vals[0], jnp.zeros_like(gidx)
        for j in range(1, EXPERTS_PER_GROUP):
            upd = vals[j] > best
            best = jnp.where(upd, vals[j], best)
            idx = jnp.where(upd, j, idx)
        return idx

    i1 = argmax4(cand)
    i2 = argmax4([jnp.where(i1 == j, -jnp.inf, cand[j]) for j in range(EXPERTS_PER_GROUP)])
    pick_raw = lambda idx: sum(jnp.where(idx == j, raw[j], 0.0) for j in range(EXPERTS_PER_GROUP))
    g1, g2 = pick_raw(i1), pick_raw(i2)
    tot = g1 + g2
    e1 = gidx * EXPERTS_PER_GROUP + i1
    e2 = gidx * EXPERTS_PER_GROUP + i2
    return e1, e2, g1 / tot, g2 / tot


def _out_ln_route_tail(mix, x_ref, g_ref, b_ref, rw_ref, rb_ref, x1_ref, xt_ref, e_ref, gt_ref):
    tm = x_ref.shape[0]
    z = DEEPNORM_ALPHA * x_ref[...] + mix
    x1 = _layer_norm(z, g_ref[...], b_ref[...])
    x1_ref[...] = x1
    _store_token_rows(xt_ref, tm, x1)
    x_hi = x1.astype(BF16)
    x_lo = (x1 - x_hi.astype(F32)).astype(BF16)
    p_hi = jnp.dot(x_hi, rw_ref[...], preferred_element_type=F32)
    p_lo = jnp.dot(x_lo, rw_ref[:, :LANES], preferred_element_type=F32)
    logits = p_hi[:, :LANES] + (p_hi[:, LANES:] + p_lo)
    logits_t = logits.T[0:N_EXPERTS, :]
    e1, e2, g1, g2 = _route(logits_t, rb_ref[...])
    e_ref[0:1, :] = e1
    e_ref[1:2, :] = e2
    gt_ref[0:1, :] = g1
    gt_ref[1:2, :] = g2


def _out_ln_route_a_kernel(a_ref, w_ref, *rest):
    mix = jnp.dot(a_ref[...], w_ref[...], preferred_element_type=F32)
    _out_ln_route_tail(mix, *rest)


def _out_ln_route_b_kernel(o0_ref, o1_ref, o2_ref, l0_ref, l1_ref, l2_ref, w_ref,
                           x_ref, g_ref, b_ref, rw_ref, rb_ref, x1_ref, xt_ref, e_ref, gt_ref,
                           os_ref, ls_ref):
    tm = x_ref.shape[0]
    merged = []
    for hd in range(HEADS_PER_BRANCH):
        for g, (o_ref, l_ref) in enumerate(((o1_ref, l1_ref), (o2_ref, l2_ref))):
            d = BRANCHES[g + 1][1]
            for r in range(d):
                cols = slice(r * BRANCH_WIDTH + hd * HEAD_DIM, r * BRANCH_WIDTH + (hd + 1) * HEAD_DIM)
                os_ref.at[g, hd][pl.ds(r, tm // d, stride=d), :] = o_ref[:, cols].astype(F32)
                ls_ref.at[g, hd][pl.ds(r, tm // d, stride=d), :] = l_ref[:, cols]
        cols = slice(hd * HEAD_DIM, (hd + 1) * HEAD_DIM)
        l0, l1, l2 = l0_ref[:, cols], ls_ref[0, hd], ls_ref[1, hd]
        lm = jnp.maximum(jnp.maximum(l0, l1), l2)
        w0, w1, w2 = jnp.exp(l0 - lm), jnp.exp(l1 - lm), jnp.exp(l2 - lm)
        m = (o0_ref[:, cols].astype(F32) * w0 + os_ref[0, hd] * w1 + os_ref[1, hd] * w2) / (w0 + w1 + w2)
        merged.append(m.astype(BF16))
    mix = jnp.dot(jnp.concatenate(merged, axis=1), w_ref[...], preferred_element_type=F32)
    _out_ln_route_tail(mix, x_ref, g_ref, b_ref, rw_ref, rb_ref, x1_ref, xt_ref, e_ref, gt_ref)


def _out_ln_route(lhs, w_bf, x, ln_g, ln_b, rw_split, rb_col):
    tm = OUT_TM
    kdim = w_bf.shape[0]
    row = lambda width: pl.BlockSpec((tm, width), lambda i: (i, 0))
    full = lambda shape: pl.BlockSpec(shape, lambda i: (0,) * len(shape))
    lane_row = pl.BlockSpec((TOP_K, tm), lambda i: (0, i))
    if len(lhs) == 1:
        body, name, lhs_specs, scratch = _out_ln_route_a_kernel, "out_ln_route_conv", [row(kdim)], []
    else:
        body, name = _out_ln_route_b_kernel, "out_ln_route_attn"
        blocked = [pl.BlockSpec((tm // d, d * BRANCH_WIDTH), lambda i: (i, 0)) for _, d in BRANCHES]
        lhs_specs = blocked + blocked
        scratch = [pltpu.VMEM((N_BRANCH - 1, HEADS_PER_BRANCH, tm, HEAD_DIM), F32)] * 2
    return pl.pallas_call(
        body,
        out_shape=(jax.ShapeDtypeStruct((N_TOK, D_MODEL), F32),
                   jax.ShapeDtypeStruct((N_TOK * PACK_ROWS, LANES), F32),
                   jax.ShapeDtypeStruct((TOP_K, N_TOK), jnp.int32),
                   jax.ShapeDtypeStruct((TOP_K, N_TOK), F32)),
        grid=(N_TOK // tm,),
        in_specs=lhs_specs + [
            full((kdim, D_MODEL)), row(D_MODEL), full((1, D_MODEL)), full((1, D_MODEL)),
            full((D_MODEL, 2 * LANES)), full((N_EXPERTS, 1)),
        ],
        out_specs=(row(D_MODEL), pl.BlockSpec((tm * PACK_ROWS, LANES), lambda i: (i, 0)), lane_row, lane_row),
        scratch_shapes=scratch,
        compiler_params=_params(1),
        name=name,
    )(*lhs, w_bf, x, ln_g.reshape(1, D_MODEL), ln_b.reshape(1, D_MODEL), rw_split, rb_col)


def _build_plan(eidx, gates):
    flat_e = eidx.reshape(-1)
    onehot = (flat_e[:, None] == jnp.arange(N_EXPERTS, dtype=jnp.int32)[None, :]).astype(jnp.int32)
    csum = jnp.cumsum(onehot, axis=0)
    rank = jnp.sum(onehot * csum, axis=1) - 1
    counts = csum[-1]
    nblk_e = ((counts + MOE_BLK - 1) // MOE_BLK).astype(jnp.int32)
    blk_end = jnp.cumsum(nblk_e).astype(jnp.int32)
    blk_start = blk_end - nblk_e
    dest = (blk_start[flat_e] * MOE_BLK + rank).astype(jnp.int32)
    n_valid = blk_end[-1]

    chunks = jnp.full((1,), MOE_CHUNKS, jnp.int32)
    phase_len = jnp.concatenate([chunks, jnp.maximum(nblk_e[:-1], MOE_CHUNKS), nblk_e[-1:]])
    phase_end = jnp.cumsum(phase_len).astype(jnp.int32)
    phase_start = phase_end - phase_len
    step = jnp.arange(MOE_STEPS, dtype=jnp.int32)
    ph = jnp.minimum(jnp.sum((step[:, None] >= phase_end[None, :]).astype(jnp.int32), axis=1), N_EXPERTS)
    off = step - phase_start[ph]
    live = step < phase_end[-1]
    ce = jnp.clip(ph - 1, 0, N_EXPERTS - 1)
    nb_c = jnp.where(ph >= 1, nblk_e[ce], 0)
    comp_on = (live & (ph >= 1) & (off < nb_c)).astype(jnp.int32)
    comp_blk = jnp.clip(jnp.where(ph >= 1, blk_start[ce], 0) + jnp.minimum(off, nb_c - 1), 0, n_valid - 1)
    load_on = (live & (ph < N_EXPERTS) & (off < MOE_CHUNKS)).astype(jnp.int32)
    load_e = jnp.minimum(ph, N_EXPERTS - 1)
    load_c = jnp.where(ph < N_EXPERTS, jnp.minimum(off, MOE_CHUNKS - 1), MOE_CHUNKS - 1)
    sched = (comp_on, comp_blk.astype(jnp.int32), (ph + 1) % 2, load_on, load_e, load_c.astype(jnp.int32), ph % 2)
    return dest[:N_TOK], dest[N_TOK:], blk_end - 1, nblk_e, sched, gates.T


def _dispatch_kernel(d0_ref, d1_ref, last_ref, nblk_ref, xp_ref, xs_hbm, zero_ref, zsem, sem):
    i = pl.program_id(0)
    tm = xp_ref.shape[0] // PACK_ROWS
    blk_rows = MOE_BLK * PACK_ROWS

    @pl.when(i == 0)
    def _():
        zero_ref[...] = jnp.zeros_like(zero_ref)

        def zero_copy(e):
            start = pl.multiple_of(last_ref[e] * blk_rows, blk_rows)
            return pltpu.make_async_copy(zero_ref, xs_hbm.at[pl.ds(start, blk_rows)], zsem)

        for e in range(N_EXPERTS):
            @pl.when(nblk_ref[e] > 0)
            def _():
                zero_copy(e).start()
        for e in range(N_EXPERTS):
            @pl.when(nblk_ref[e] > 0)
            def _():
                zero_copy(e).wait()

    def body(r, carry):
        t = i * tm + r
        src = xp_ref.at[pl.ds(pl.multiple_of(r * PACK_ROWS, PACK_ROWS), PACK_ROWS)]
        for k, d_ref in enumerate((d0_ref, d1_ref)):
            dst = xs_hbm.at[pl.ds(pl.multiple_of(d_ref[t] * PACK_ROWS, PACK_ROWS), PACK_ROWS)]
            pltpu.make_async_copy(src, dst, sem).start(priority=k)
        return carry

    lax.fori_loop(0, tm, body, 0, unroll=DMA_UNROLL)
    for _ in range(TOP_K):
        pltpu.make_async_copy(xp_ref, xs_hbm.at[pl.ds(0, tm * PACK_ROWS)], sem).wait()


def _moe_dispatch(xp, dest0, dest1, last_blk, nblk_e):
    tm = DISPATCH_TM
    return pl.pallas_call(
        _dispatch_kernel,
        out_shape=jax.ShapeDtypeStruct((MOE_ROWS * PACK_ROWS, LANES), F32),
        grid_spec=pltpu.PrefetchScalarGridSpec(
            num_scalar_prefetch=4,
            grid=(N_TOK // tm,),
            in_specs=[pl.BlockSpec((tm * PACK_ROWS, LANES), lambda i, *_: (i, 0))],
            out_specs=pl.BlockSpec(memory_space=pl.ANY),
            scratch_shapes=[pltpu.VMEM((MOE_BLK * PACK_ROWS, LANES), F32),
                            pltpu.SemaphoreType.DMA, pltpu.SemaphoreType.DMA],
        ),
        compiler_params=_params(1),
        name="moe_dispatch",
    )(dest0, dest1, last_blk, nblk_e, xp)


def _moe_ffn_kernel(con_ref, cblk_ref, cslot_ref, lon_ref, le_ref, lc_ref, lslot_ref,
                    xs_ref, w1_ref, w3_ref, w2_ref, ys_ref, w13_ref, w2b_ref):
    s = pl.program_id(0)
    blk = MOE_BLK
    n_ff = D_FF // LANES

    @pl.when(lon_ref[s] == 1)
    def _():
        slot = lslot_ref[s]
        rows = pl.ds(pl.multiple_of(lc_ref[s] * MOE_UP_ROWS, MOE_UP_ROWS), MOE_UP_ROWS)
        for j in range(n_ff):
            src = slice(j * LANES, (j + 1) * LANES)
            w13_ref[slot, rows, 2 * j * LANES:(2 * j + 1) * LANES] = w1_ref[:, src].astype(BF16)
            w13_ref[slot, rows, (2 * j + 1) * LANES:(2 * j + 2) * LANES] = w3_ref[:, src].astype(BF16)
        rows2 = pl.ds(pl.multiple_of(lc_ref[s] * MOE_DOWN_ROWS, 16), MOE_DOWN_ROWS)
        w2b_ref[slot, rows2, :] = w2_ref[...].astype(BF16)

    @pl.when(con_ref[s] == 1)
    def _():
        slot = cslot_ref[s]
        x = jnp.concatenate([p.astype(BF16) for p in _load_token_rows(xs_ref, blk)], axis=1)
        g = []
        for q in range(0, n_ff, 2):
            npair = min(2, n_ff - q)
            h = jnp.dot(x, w13_ref[slot, :, 2 * q * LANES:2 * (q + npair) * LANES], preferred_element_type=F32)
            for k in range(npair):
                h1, h3 = h[:, 2 * k * LANES:(2 * k + 1) * LANES], h[:, (2 * k + 1) * LANES:(2 * k + 2) * LANES]
                g.append((jax.nn.silu(h1) * h3).astype(BF16))
        y = jnp.dot(jnp.concatenate(g, axis=1), w2b_ref[slot], preferred_element_type=F32)
        _store_token_rows(ys_ref, blk, y)


def _moe_ffn(xs, sched, w1, w3, w2, layer):
    blk = MOE_BLK
    rows = pl.BlockSpec((blk * PACK_ROWS, LANES), lambda s, con, cblk, *_: (cblk[s], 0))
    up = pl.BlockSpec((None, None, MOE_UP_ROWS, D_FF),
                      lambda s, con, cblk, cslot, lon, le, lc, *_: (layer, le[s], lc[s], 0))
    down = pl.BlockSpec((None, None, MOE_DOWN_ROWS, D_MODEL),
                        lambda s, con, cblk, cslot, lon, le, lc, *_: (layer, le[s], lc[s], 0))
    return pl.pallas_call(
        _moe_ffn_kernel,
        out_shape=jax.ShapeDtypeStruct((MOE_ROWS * PACK_ROWS, LANES), F32),
        grid_spec=pltpu.PrefetchScalarGridSpec(
            num_scalar_prefetch=len(sched),
            grid=(MOE_STEPS,),
            in_specs=[rows, up, up, down],
            out_specs=rows,
            scratch_shapes=[pltpu.VMEM((2, D_MODEL, 2 * D_FF), BF16), pltpu.VMEM((2, D_FF, D_MODEL), BF16)],
        ),
        compiler_params=_params(1),
        name="moe_ffn",
    )(*sched, xs, w1, w3, w2)


def _combine_ln_kernel(d0_ref, d1_ref, x1_ref, gt_ref, ys_hbm, g_ref, b_ref, x2_ref, x2b_ref,
                       buf_ref, z_ref, sems):
    i = pl.program_id(0)
    tm = x2_ref.shape[0]

    def issue(tile):
        slot = tile % 2

        def body(r, carry):
            t = tile * tm + r
            for k, d_ref in enumerate((d0_ref, d1_ref)):
                src = ys_hbm.at[pl.ds(pl.multiple_of(d_ref[t] * PACK_ROWS, PACK_ROWS), PACK_ROWS)]
                dst = buf_ref.at[slot, k, pl.ds(pl.multiple_of(r * PACK_ROWS, PACK_ROWS), PACK_ROWS)]
                pltpu.make_async_copy(src, dst, sems.at[slot]).start(priority=k)
            return carry

        lax.fori_loop(0, tm, body, 0, unroll=DMA_UNROLL)

    @pl.when(i == 0)
    def _():
        issue(0)

    @pl.when(i + 1 < pl.num_programs(0))
    def _():
        issue(i + 1)

    slot = i % 2
    for k in range(TOP_K):
        pltpu.make_async_copy(ys_hbm.at[pl.ds(0, tm * PACK_ROWS)], buf_ref.at[slot, k], sems.at[slot]).wait()
    gt = gt_ref[...]
    g0, g1 = gt[:, 0:1], gt[:, 1:2]
    chunks = zip(_load_token_rows(buf_ref.at[slot, 0], tm), _load_token_rows(buf_ref.at[slot, 1], tm))
    for c, (y0, y1) in enumerate(chunks):
        z_ref[:, c * LANES:(c + 1) * LANES] = y0 * g0 + y1 * g1
    x2 = _layer_norm(DEEPNORM_ALPHA * x1_ref[...] + z_ref[...], g_ref[...], b_ref[...])
    x2_ref[...] = x2
    x2b_ref[...] = x2.astype(BF16)


def _combine_ln(x1, ys, dest0, dest1, gates_t, ln_g, ln_b):
    tm = COMB_TM
    row = lambda width: pl.BlockSpec((tm, width), lambda i, d0, d1: (i, 0))
    vec = pl.BlockSpec((1, D_MODEL), lambda i, d0, d1: (0, 0))
    return pl.pallas_call(
        _combine_ln_kernel,
        out_shape=(jax.ShapeDtypeStruct((N_TOK, D_MODEL), F32),
                   jax.ShapeDtypeStruct((N_TOK, D_MODEL), BF16)),
        grid_spec=pltpu.PrefetchScalarGridSpec(
            num_scalar_prefetch=2,
            grid=(N_TOK // tm,),
            in_specs=[row(D_MODEL), row(TOP_K), pl.BlockSpec(memory_space=pl.ANY), vec, vec],
            out_specs=(row(D_MODEL), row(D_MODEL)),
            scratch_shapes=[pltpu.VMEM((2, TOP_K, tm * PACK_ROWS, LANES), F32),
                            pltpu.VMEM((tm, D_MODEL), F32),
                            pltpu.SemaphoreType.DMA((2,))],
        ),
        compiler_params=_params(1),
        name="combine_ln",
    )(dest0, dest1, x1, gates_t, ys, ln_g.reshape(1, D_MODEL), ln_b.reshape(1, D_MODEL))


def kernel(x, a_w_in, a_conv_w, a_w_out, kv_w, b_w_q, b_w_o, router_w, router_bias,
           moe_w1, moe_w3, moe_w2, ln1_g, ln1_b, ln2_g, ln2_b):
    x = x.reshape(N_TOK, D_MODEL)
    xb = x.astype(BF16)
    rw_pad = jnp.pad(router_w.astype(F32), ((0, 0), (0, LANES - N_EXPERTS)))
    rw_hi = rw_pad.astype(BF16)
    rw_lo = (rw_pad - rw_hi.astype(F32)).astype(BF16)
    rw_split = jnp.concatenate([rw_hi, rw_lo], axis=1)
    rb_col = router_bias.astype(F32).reshape(N_EXPERTS, 1)
    k_tables = _rope_tables(1.0)
    q_tables = _rope_tables(1.0 / math.sqrt(HEAD_DIM))
    dils = [d for _, d in BRANCHES]
    k_sh = v_sh = None
    for i in range(DEPTH):
        if i < N_A_LAYERS:
            y = _conv_proj(xb, a_w_in, a_conv_w, i)
            lhs, w_out = [y], a_w_out[i].astype(BF16)
        else:
            j = i - N_A_LAYERS
            if k_sh is None:
                kv3 = kv_w[None]
                k_sh = [_proj(xb, kv3, 0, g, dils[g], k_tables, rope=True) for g in range(N_BRANCH)]
                v_sh = [_proj(xb, kv3, 0, N_BRANCH + g, dils[g], k_tables, rope=False) for g in range(N_BRANCH)]
            q = [_proj(xb, b_w_q, j, g, dils[g], q_tables, rope=True) for g in range(N_BRANCH)]
            outs = [_attn_branch(q[g], k_sh[g], v_sh[g], dils[g]) for g in range(N_BRANCH)]
            lhs, w_out = [o for o, _ in outs] + [l for _, l in outs], b_w_o[j].astype(BF16)
        x1, xt, eidx, gates = _out_ln_route(lhs, w_out, x, ln1_g[i], ln1_b[i], rw_split, rb_col)
        dest0, dest1, last_blk, nblk_e, sched, gates_t = _build_plan(eidx, gates)
        xs = _moe_dispatch(xt, dest0, dest1, last_blk, nblk_e)
        ys = _moe_ffn(xs, sched, moe_w1, moe_w3, moe_w2, i)
        x, xb = _combine_ln(x1, ys, dest0, dest1, gates_t, ln2_g[i], ln2_b[i])
    return x.reshape(BATCH, SEQ, D_MODEL)
```

```python
import functools
import math

import jax
import jax.numpy as jnp
from jax import lax
from jax.experimental import pallas as pl
from jax.experimental.pallas import tpu as pltpu

D_MODEL = 2048
BATCH = 4
SEQ = 4096
DEPTH = 4
N_TOK = BATCH * SEQ
N_A_LAYERS = DEPTH // 2
CONV_WIDTH = 3
BRANCHES = ((128, 1), (512, 4), (2048, 16))
N_BRANCH = len(BRANCHES)
HEADS_PER_BRANCH = 8
HEAD_DIM = 128
BRANCH_WIDTH = HEADS_PER_BRANCH * HEAD_DIM
SPAN = 128
ROT_DIM = HEAD_DIM // 4
ROPE_THETA = 500000.0
N_EXPERTS = 16
N_GROUPS = 4
EXPERTS_PER_GROUP = N_EXPERTS // N_GROUPS
TOP_K = 2
D_FF = 1408
DEEPNORM_ALPHA = (2.0 * DEPTH) ** 0.25
LN_EPS = 1e-5

F32 = jnp.float32
BF16 = jnp.bfloat16

VMEM_LIMIT_BYTES = 56 * 1024 * 1024
LANES = 128
TOK_ROWS = D_MODEL // LANES
TOK_PITCH = TOK_ROWS + 4

CONV_TM, CONV_TN = 2048, 256
PROJ_TM = 1024
OUT_TM = 256
ATTN_QB = 512
MOE_BLK = 256
MOE_NBLK = (N_TOK * TOP_K) // MOE_BLK + N_EXPERTS
MOE_ROWS = MOE_NBLK * MOE_BLK
MOE_CHUNKS = 8
MOE_UP_ROWS = D_MODEL // MOE_CHUNKS
MOE_DOWN_ROWS = D_FF // MOE_CHUNKS
MOE_STEPS = MOE_NBLK + N_EXPERTS * MOE_CHUNKS
DISPATCH_TM = 512
COMB_TM = 512
DMA_UNROLL = 8


def _params(n_axes):
    return pltpu.CompilerParams(
        dimension_semantics=("arbitrary",) * n_axes,
        vmem_limit_bytes=VMEM_LIMIT_BYTES,
    )


def _store_token_rows(dst_ref, rows, v):
    for c in range(TOK_ROWS):
        dst_ref[pl.ds(c, rows, stride=TOK_PITCH), :] = v[:, c * LANES:(c + 1) * LANES]


def _load_token_rows(src_ref, rows):
    return [src_ref[pl.ds(c, rows, stride=TOK_PITCH), :] for c in range(TOK_ROWS)]


def _token_slot(t):
    return pl.ds(pl.multiple_of(t * TOK_PITCH, 4), TOK_ROWS)


def _conv_proj_kernel(x_ref, wb_ref, wc_ref, wh_ref, cw_ref, y_ref, wbf_ref, ext_ref):
    i = pl.program_id(1)
    tm = x_ref.shape[0]

    @pl.when(i == 0)
    def _():
        wbf_ref[0] = wb_ref[...].astype(BF16)
        wbf_ref[1] = wc_ref[...].astype(BF16)
        wbf_ref[2] = wh_ref[...].astype(BF16)

    @pl.when(i % (SEQ // tm) == 0)
    def _():
        ext_ref[0:8, :] = jnp.zeros((8, ext_ref.shape[1]), F32)

    cw = cw_ref[...]
    x = x_ref[...]
    gate_b = jnp.dot(x, wbf_ref[0], preferred_element_type=F32)
    gate_c = jnp.dot(x, wbf_ref[1], preferred_element_type=F32)
    h = jnp.dot(x, wbf_ref[2], preferred_element_type=F32)
    u = gate_c * h
    ext_ref[8:8 + tm, :] = u
    u1 = ext_ref[7:7 + tm, :]
    u2 = ext_ref[6:6 + tm, :]
    conv = cw[2:3, :] * u + cw[1:2, :] * u1 + cw[0:1, :] * u2
    y_ref[...] = (gate_b * conv).astype(BF16)
    ext_ref[0:8, :] = ext_ref[tm:tm + 8, :]


def _conv_proj(xb, w_in, conv_w, layer):
    tm, tn = CONV_TM, CONV_TN
    nj = D_MODEL // tn
    wspec = lambda off: pl.BlockSpec((None, D_MODEL, tn), lambda j, i: (layer, 0, j + off * nj))
    return pl.pallas_call(
        _conv_proj_kernel,
        out_shape=jax.ShapeDtypeStruct((N_TOK, D_MODEL), BF16),
        grid=(nj, N_TOK // tm),
        in_specs=[
            pl.BlockSpec((tm, D_MODEL), lambda j, i: (i, 0)),
            wspec(0), wspec(1), wspec(2),
            pl.BlockSpec((None, CONV_WIDTH, tn), lambda j, i: (layer, 0, j)),
        ],
        out_specs=pl.BlockSpec((tm, tn), lambda j, i: (i, j)),
        scratch_shapes=[pltpu.VMEM((3, D_MODEL, tn), BF16), pltpu.VMEM((tm + 8, tn), F32)],
        compiler_params=_params(2),
        name="conv_proj",
    )(xb, w_in, w_in, w_in, conv_w)


def _proj_kernel(x_ref, w_ref, c_ref, s1_ref, s2_ref, o_ref, wbf_ref, rot_ref, *, rope, dilation):
    tm = x_ref.shape[0]

    @pl.when(pl.program_id(0) == 0)
    def _():
        wbf_ref[...] = w_ref[...].astype(BF16)

    acc = jnp.dot(x_ref[...], wbf_ref[...], preferred_element_type=F32)
    if rope:
        c = c_ref[...]
        s1 = s1_ref[...]
        s2 = s2_ref[...]
    for hd in range(HEADS_PER_BRANCH):
        sl = slice(hd * HEAD_DIM, (hd + 1) * HEAD_DIM)
        t = acc[:, sl]
        if rope:
            t = t * c + pltpu.roll(t, HEAD_DIM - ROT_DIM // 2, 1) * s1 + pltpu.roll(t, ROT_DIM // 2, 1) * s2
        if dilation == 1:
            o_ref[:, sl] = t.astype(BF16)
        else:
            rot_ref[hd] = t
            for r in range(dilation):
                piece = rot_ref.at[hd][pl.ds(r, tm // dilation, stride=dilation), :]
                o_ref[:, r * BRANCH_WIDTH + hd * HEAD_DIM:r * BRANCH_WIDTH + (hd + 1) * HEAD_DIM] = piece.astype(BF16)


def _proj(xb, w3d, layer, col, dilation, tables, rope):
    tm = PROJ_TM
    w = BRANCH_WIDTH
    tab = pl.BlockSpec((tm, HEAD_DIM), lambda i: (i % (SEQ // tm), 0))
    return pl.pallas_call(
        functools.partial(_proj_kernel, rope=rope, dilation=dilation),
        out_shape=jax.ShapeDtypeStruct((N_TOK // dilation, dilation * w), BF16),
        grid=(N_TOK // tm,),
        in_specs=[
            pl.BlockSpec((tm, D_MODEL), lambda i: (i, 0)),
            pl.BlockSpec((None, D_MODEL, w), lambda i: (layer, 0, col)),
            tab, tab, tab,
        ],
        out_specs=pl.BlockSpec((tm // dilation, dilation * w), lambda i: (i, 0)),
        scratch_shapes=[pltpu.VMEM((D_MODEL, w), BF16), pltpu.VMEM((HEADS_PER_BRANCH, tm, HEAD_DIM), F32)],
        compiler_params=_params(1),
        name=("proj_rope" if rope else "proj") + f"_d{dilation}",
    )(xb, w3d, *tables)


def _rope_tables(scale):
    inv_freq = ROPE_THETA ** (-jnp.arange(0, ROT_DIM, 2, dtype=F32) / ROT_DIM)
    ang = jnp.arange(SEQ, dtype=F32)[:, None] * inv_freq[None, :]
    cos, sin = jnp.cos(ang), jnp.sin(ang)
    half = ROT_DIM // 2
    rest = HEAD_DIM - ROT_DIM
    c = jnp.concatenate([cos, cos, jnp.ones((SEQ, rest), F32)], axis=1) * scale
    s1 = jnp.concatenate([-sin, jnp.zeros((SEQ, HEAD_DIM - half), F32)], axis=1) * scale
    s2 = jnp.concatenate([jnp.zeros((SEQ, half), F32), sin, jnp.zeros((SEQ, rest), F32)], axis=1) * scale
    return c, s1, s2


def _attn_kernel(q_ref, kc_ref, kp_ref, vc_ref, vp_ref, o_ref, l_ref):
    qb = q_ref.shape[0]
    qi = lax.broadcasted_iota(jnp.int32, (SPAN, 2 * SPAN), 0)
    kj = lax.broadcasted_iota(jnp.int32, (SPAN, 2 * SPAN), 1)
    dist = SPAN + qi - kj
    band = (dist >= 0) & (dist <= SPAN)
    first_key = jnp.where(pl.program_id(1) == 0, SPAN, 0)
    band_first = band & (kj >= first_key)
    for a in range(qb // SPAN):
        rows = slice(a * SPAN, (a + 1) * SPAN)
        for hd in range(HEADS_PER_BRANCH):
            cols = slice(hd * HEAD_DIM, (hd + 1) * HEAD_DIM)
            q = q_ref[rows, cols]
            if a == 0:
                k_prev, v_prev = kp_ref[:, cols], vp_ref[:, cols]
                mask = band_first
            else:
                prows = slice((a - 1) * SPAN, a * SPAN)
                k_prev, v_prev = kc_ref[prows, cols], vc_ref[prows, cols]
                mask = band
            kk = jnp.concatenate([k_prev, kc_ref[rows, cols]], axis=0)
            vv = jnp.concatenate([v_prev, vc_ref[rows, cols]], axis=0)
            s = lax.dot_general(q, kk, (((1,), (1,)), ((), ())), preferred_element_type=F32)
            s = jnp.where(mask, s, -jnp.inf)
            m = jnp.max(s, axis=-1, keepdims=True)
            p = jnp.exp(s - m)
            den = jnp.sum(p, axis=-1, keepdims=True)
            o = jnp.dot(p.astype(BF16), vv, preferred_element_type=F32) / den
            o_ref[rows, cols] = o.astype(BF16)
            l_ref[rows, cols] = jnp.broadcast_to(m + jnp.log(den), (SPAN, HEAD_DIM))


def _attn_branch(q, k, v, dilation):
    sub = SEQ // dilation
    qb = min(ATTN_QB, sub)
    w = BRANCH_WIDTH
    view = lambda t: t.reshape(BATCH, sub, dilation * w)
    cur = pl.BlockSpec((None, qb, w), lambda b, n, r: (b, n, r))
    prev = pl.BlockSpec((None, SPAN, w),
                        lambda b, n, r: (b, jnp.maximum(n * (qb // SPAN) - 1, 0), r))
    o, l = pl.pallas_call(
        _attn_kernel,
        out_shape=(jax.ShapeDtypeStruct((BATCH, sub, dilation * w), BF16),
                   jax.ShapeDtypeStruct((BATCH, sub, dilation * w), F32)),
        grid=(BATCH, sub // qb, dilation),
        in_specs=[cur, cur, prev, cur, prev],
        out_specs=(cur, cur),
        compiler_params=_params(3),
        name=f"dilated_attn_d{dilation}",
    )(view(q), view(k), view(k), view(v), view(v))
    return o.reshape(N_TOK // dilation, dilation * w), l.reshape(N_TOK // dilation, dilation * w)


def _layer_norm(z, g, b):
    mean = jnp.mean(z, axis=-1, keepdims=True)
    zc = z - mean
    var = jnp.mean(zc * zc, axis=-1, keepdims=True)
    return zc * lax.rsqrt(var + LN_EPS) * g + b


def _route(logits_t, bias_col):
    scores = jax.nn.sigmoid(logits_t)
    sel = scores + bias_col
    sel_r = [sel[e:e + 1, :] for e in range(N_EXPERTS)]
    sc_r = [scores[e:e + 1, :] for e in range(N_EXPERTS)]
    best_g = None
    for g in range(N_GROUPS):
        v = sel_r[g * EXPERTS_PER_GROUP:(g + 1) * EXPERTS_PER_GROUP]
        gs = None
        for a in range(EXPERTS_PER_GROUP):
            for b in range(a + 1, EXPERTS_PER_GROUP):
                ps = v[a] + v[b]
                gs = ps if gs is None else jnp.maximum(gs, ps)
        if best_g is None:
            best_g, gidx = gs, jnp.zeros_like(gs, dtype=jnp.int32)
        else:
            upd = gs > best_g
            best_g = jnp.where(upd, gs, best_g)
            gidx = jnp.where(upd, g, gidx)
    cand, raw = [], []
    for j in range(EXPERTS_PER_GROUP):
        cv, rv = sel_r[j], sc_r[j]
        for g in range(1, N_GROUPS):
            pick = gidx == g
            cv = jnp.where(pick, sel_r[g * EXPERTS_PER_GROUP + j], cv)
            rv = jnp.where(pick, sc_r[g * EXPERTS_PER_GROUP + j], rv)
        cand.append(cv)
        raw.append(rv)

    def argmax4(vals):
        best, idx = vals[0], jnp.zeros_like(gidx)
        for j in range(1, EXPERTS_PER_GROUP):
            upd = vals[j] > best
            best = jnp.where(upd, vals[j], best)
            idx = jnp.where(upd, j, idx)
        return idx

    i1 = argmax4(cand)
    i2 = argmax4([jnp.where(i1 == j, -jnp.inf, cand[j]) for j in range(EXPERTS_PER_GROUP)])
    pick_raw = lambda idx: sum(jnp.where(idx == j, raw[j], 0.0) for j in range(EXPERTS_PER_GROUP))
    g1, g2 = pick_raw(i1), pick_raw(i2)
    tot = g1 + g2
    e1 = gidx * EXPERTS_PER_GROUP + i1
    e2 = gidx * EXPERTS_PER_GROUP + i2
    return e1, e2, g1 / tot, g2 / tot


def _out_ln_route_tail(mix, x_ref, g_ref, b_ref, rw_ref, rb_ref, x1_ref, xt_ref, e_ref, gt_ref):
    tm = x_ref.shape[0]
    z = DEEPNORM_ALPHA * x_ref[...] + mix
    x1 = _layer_norm(z, g_ref[...], b_ref[...])
    x1_ref[...] = x1
    _store_token_rows(xt_ref, tm, x1)
    x_hi = x1.astype(BF16)
    x_lo = (x1 - x_hi.astype(F32)).astype(BF16)
    p_hi = jnp.dot(x_hi, rw_ref[...], preferred_element_type=F32)
    p_lo = jnp.dot(x_lo, rw_ref[:, :LANES], preferred_element_type=F32)
    logits = p_hi[:, :LANES] + (p_hi[:, LANES:] + p_lo)
    logits_t = logits.T[0:N_EXPERTS, :]
    e1, e2, g1, g2 = _route(logits_t, rb_ref[...])
    e_ref[0:1, :] = e1
    e_ref[1:2, :] = e2
    gt_ref[0:1, :] = g1
    gt_ref[1:2, :] = g2


def _out_ln_route_a_kernel(a_ref, w_ref, *rest):
    mix = jnp.dot(a_ref[...], w_ref[...], preferred_element_type=F32)
    _out_ln_route_tail(mix, *rest)


def _out_ln_route_b_kernel(o0_ref, o1_ref, o2_ref, l0_ref, l1_ref, l2_ref, w_ref,
                           x_ref, g_ref, b_ref, rw_ref, rb_ref, x1_ref, xt_ref, e_ref, gt_ref,
                           os_ref, ls_ref):
    tm = x_ref.shape[0]
    merged = []
    for hd in range(HEADS_PER_BRANCH):
        for g, (o_ref, l_ref) in enumerate(((o1_ref, l1_ref), (o2_ref, l2_ref))):
            d = BRANCHES[g + 1][1]
            for r in range(d):
                cols = slice(r * BRANCH_WIDTH + hd * HEAD_DIM, r * BRANCH_WIDTH + (hd + 1) * HEAD_DIM)
                os_ref.at[g, hd][pl.ds(r, tm // d, stride=d), :] = o_ref[:, cols].astype(F32)
                ls_ref.at[g, hd][pl.ds(r, tm // d, stride=d), :] = l_ref[:, cols]
        cols = slice(hd * HEAD_DIM, (hd + 1) * HEAD_DIM)
        l0, l1, l2 = l0_ref[:, cols], ls_ref[0, hd], ls_ref[1, hd]
        lm = jnp.maximum(jnp.maximum(l0, l1), l2)
        w0, w1, w2 = jnp.exp(l0 - lm), jnp.exp(l1 - lm), jnp.exp(l2 - lm)
        m = (o0_ref[:, cols].astype(F32) * w0 + os_ref[0, hd] * w1 + os_ref[1, hd] * w2) / (w0 + w1 + w2)
        merged.append(m.astype(BF16))
    mix = jnp.dot(jnp.concatenate(merged, axis=1), w_ref[...], preferred_element_type=F32)
    _out_ln_route_tail(mix, x_ref, g_ref, b_ref, rw_ref, rb_ref, x1_ref, xt_ref, e_ref, gt_ref)


def _out_ln_route(lhs, w_bf, x, ln_g, ln_b, rw_split, rb_col):
    tm = OUT_TM
    kdim = w_bf.shape[0]
    row = lambda width: pl.BlockSpec((tm, width), lambda i: (i, 0))
    full = lambda shape: pl.BlockSpec(shape, lambda i: (0,) * len(shape))
    lane_row = pl.BlockSpec((TOP_K, tm), lambda i: (0, i))
    if len(lhs) == 1:
        body, name, lhs_specs, scratch = _out_ln_route_a_kernel, "out_ln_route_conv", [row(kdim)], []
    else:
        body, name = _out_ln_route_b_kernel, "out_ln_route_attn"
        blocked = [pl.BlockSpec((tm // d, d * BRANCH_WIDTH), lambda i: (i, 0)) for _, d in BRANCHES]
        lhs_specs = blocked + blocked
        scratch = [pltpu.VMEM((N_BRANCH - 1, HEADS_PER_BRANCH, tm, HEAD_DIM), F32)] * 2
    return pl.pallas_call(
        body,
        out_shape=(jax.ShapeDtypeStruct((N_TOK, D_MODEL), F32),
                   jax.ShapeDtypeStruct((N_TOK * TOK_PITCH, LANES), F32),
                   jax.ShapeDtypeStruct((TOP_K, N_TOK), jnp.int32),
                   jax.ShapeDtypeStruct((TOP_K, N_TOK), F32)),
        grid=(N_TOK // tm,),
        in_specs=lhs_specs + [
            full((kdim, D_MODEL)), row(D_MODEL), full((1, D_MODEL)), full((1, D_MODEL)),
            full((D_MODEL, 2 * LANES)), full((N_EXPERTS, 1)),
        ],
        out_specs=(row(D_MODEL), pl.BlockSpec((tm * TOK_PITCH, LANES), lambda i: (i, 0)), lane_row, lane_row),
        scratch_shapes=scratch,
        compiler_params=_params(1),
        name=name,
    )(*lhs, w_bf, x, ln_g.reshape(1, D_MODEL), ln_b.reshape(1, D_MODEL), rw_split, rb_col)


def _build_plan(eidx, gates):
    flat_e = eidx.reshape(-1)
    onehot = (flat_e[:, None] == jnp.arange(N_EXPERTS, dtype=jnp.int32)[None, :]).astype(jnp.int32)
    csum = jnp.cumsum(onehot, axis=0)
    rank = jnp.sum(onehot * csum, axis=1) - 1
    counts = csum[-1]
    nblk_e = ((counts + MOE_BLK - 1) // MOE_BLK).astype(jnp.int32)
    blk_end = jnp.cumsum(nblk_e).astype(jnp.int32)
    blk_start = blk_end - nblk_e
    dest = (blk_start[flat_e] * MOE_BLK + rank).astype(jnp.int32)
    n_valid = blk_end[-1]

    chunks = jnp.full((1,), MOE_CHUNKS, jnp.int32)
    phase_len = jnp.concatenate([chunks, jnp.maximum(nblk_e[:-1], MOE_CHUNKS), nblk_e[-1:]])
    phase_end = jnp.cumsum(phase_len).astype(jnp.int32)
    phase_start = phase_end - phase_len
    step = jnp.arange(MOE_STEPS, dtype=jnp.int32)
    ph = jnp.minimum(jnp.sum((step[:, None] >= phase_end[None, :]).astype(jnp.int32), axis=1), N_EXPERTS)
    off = step - phase_start[ph]
    live = step < phase_end[-1]
    ce = jnp.clip(ph - 1, 0, N_EXPERTS - 1)
    nb_c = jnp.where(ph >= 1, nblk_e[ce], 0)
    comp_on = (live & (ph >= 1) & (off < nb_c)).astype(jnp.int32)
    comp_blk = jnp.clip(jnp.where(ph >= 1, blk_start[ce], 0) + jnp.minimum(off, nb_c - 1), 0, n_valid - 1)
    load_on = (live & (ph < N_EXPERTS) & (off < MOE_CHUNKS)).astype(jnp.int32)
    load_e = jnp.minimum(ph, N_EXPERTS - 1)
    load_c = jnp.where(ph < N_EXPERTS, jnp.minimum(off, MOE_CHUNKS - 1), MOE_CHUNKS - 1)
    sched = (comp_on, comp_blk.astype(jnp.int32), (ph + 1) % 2, load_on, load_e, load_c.astype(jnp.int32), ph % 2)
    return dest[:N_TOK], dest[N_TOK:], blk_end - 1, nblk_e, sched, gates.T


def _dispatch_kernel(d0_ref, d1_ref, last_ref, nblk_ref, xp_ref, xs_hbm, zero_ref, zsem, sem):
    i = pl.program_id(0)
    tm = xp_ref.shape[0] // TOK_PITCH
    blk_rows = MOE_BLK * TOK_PITCH

    @pl.when(i == 0)
    def _():
        zero_ref[...] = jnp.zeros_like(zero_ref)

        def zero_copy(e):
            start = pl.multiple_of(last_ref[e] * blk_rows, blk_rows)
            return pltpu.make_async_copy(zero_ref, xs_hbm.at[pl.ds(start, blk_rows)], zsem)

        for e in range(N_EXPERTS):
            @pl.when(nblk_ref[e] > 0)
            def _():
                zero_copy(e).start()
        for e in range(N_EXPERTS):
            @pl.when(nblk_ref[e] > 0)
            def _():
                zero_copy(e).wait()

    def body(r, carry):
        t = i * tm + r
        src = xp_ref.at[_token_slot(r)]
        for k, d_ref in enumerate((d0_ref, d1_ref)):
            dst = xs_hbm.at[_token_slot(d_ref[t])]
            pltpu.make_async_copy(src, dst, sem).start(priority=k)
        return carry

    lax.fori_loop(0, tm, body, 0, unroll=DMA_UNROLL)
    for _ in range(TOP_K):
        whole = pl.ds(0, tm * TOK_ROWS)
        pltpu.make_async_copy(xp_ref.at[whole], xs_hbm.at[whole], sem).wait()


def _moe_dispatch(xp, dest0, dest1, last_blk, nblk_e):
    tm = DISPATCH_TM
    return pl.pallas_call(
        _dispatch_kernel,
        out_shape=jax.ShapeDtypeStruct((MOE_ROWS * TOK_PITCH, LANES), F32),
        grid_spec=pltpu.PrefetchScalarGridSpec(
            num_scalar_prefetch=4,
            grid=(N_TOK // tm,),
            in_specs=[pl.BlockSpec((tm * TOK_PITCH, LANES), lambda i, *_: (i, 0))],
            out_specs=pl.BlockSpec(memory_space=pl.ANY),
            scratch_shapes=[pltpu.VMEM((MOE_BLK * TOK_PITCH, LANES), F32),
                            pltpu.SemaphoreType.DMA, pltpu.SemaphoreType.DMA],
        ),
        compiler_params=_params(1),
        name="moe_dispatch",
    )(dest0, dest1, last_blk, nblk_e, xp)


def _moe_ffn_kernel(con_ref, cblk_ref, cslot_ref, lon_ref, le_ref, lc_ref, lslot_ref,
                    xs_ref, w1_ref, w3_ref, w2_ref, ys_ref, w13_ref, w2b_ref):
    s = pl.program_id(0)
    blk = MOE_BLK
    n_ff = D_FF // LANES

    @pl.when(lon_ref[s] == 1)
    def _():
        slot = lslot_ref[s]
        rows = pl.ds(pl.multiple_of(lc_ref[s] * MOE_UP_ROWS, MOE_UP_ROWS), MOE_UP_ROWS)
        for j in range(n_ff):
            src = slice(j * LANES, (j + 1) * LANES)
            w13_ref[slot, rows, 2 * j * LANES:(2 * j + 1) * LANES] = w1_ref[:, src].astype(BF16)
            w13_ref[slot, rows, (2 * j + 1) * LANES:(2 * j + 2) * LANES] = w3_ref[:, src].astype(BF16)
        rows2 = pl.ds(pl.multiple_of(lc_ref[s] * MOE_DOWN_ROWS, 16), MOE_DOWN_ROWS)
        w2b_ref[slot, rows2, :] = w2_ref[...].astype(BF16)

    @pl.when(con_ref[s] == 1)
    def _():
        slot = cslot_ref[s]
        x = jnp.concatenate([p.astype(BF16) for p in _load_token_rows(xs_ref, blk)], axis=1)
        g = []
        for q in range(0, n_ff, 2):
            npair = min(2, n_ff - q)
            h = jnp.dot(x, w13_ref[slot, :, 2 * q * LANES:2 * (q + npair) * LANES], preferred_element_type=F32)
            for k in range(npair):
                h1, h3 = h[:, 2 * k * LANES:(2 * k + 1) * LANES], h[:, (2 * k + 1) * LANES:(2 * k + 2) * LANES]
                g.append((jax.nn.silu(h1) * h3).astype(BF16))
        y = jnp.dot(jnp.concatenate(g, axis=1), w2b_ref[slot], preferred_element_type=F32)
        _store_token_rows(ys_ref, blk, y)


def _moe_ffn(xs, sched, w1, w3, w2, layer):
    blk = MOE_BLK
    rows = pl.BlockSpec((blk * TOK_PITCH, LANES), lambda s, con, cblk, *_: (cblk[s], 0))
    up = pl.BlockSpec((None, None, MOE_UP_ROWS, D_FF),
                      lambda s, con, cblk, cslot, lon, le, lc, *_: (layer, le[s], lc[s], 0))
    down = pl.BlockSpec((None, None, MOE_DOWN_ROWS, D_MODEL),
                        lambda s, con, cblk, cslot, lon, le, lc, *_: (layer, le[s], lc[s], 0))
    return pl.pallas_call(
        _moe_ffn_kernel,
        out_shape=jax.ShapeDtypeStruct((MOE_ROWS * TOK_PITCH, LANES), F32),
        grid_spec=pltpu.PrefetchScalarGridSpec(
            num_scalar_prefetch=len(sched),
            grid=(MOE_STEPS,),
            in_specs=[rows, up, up, down],
            out_specs=rows,
            scratch_shapes=[pltpu.VMEM((2, D_MODEL, 2 * D_FF), BF16), pltpu.VMEM((2, D_FF, D_MODEL), BF16)],
        ),
        compiler_params=_params(1),
        name="moe_ffn",
    )(*sched, xs, w1, w3, w2)


def _combine_ln_kernel(d0_ref, d1_ref, x1_ref, gt_ref, ys_hbm, g_ref, b_ref, x2_ref, x2b_ref,
                       buf_ref, z_ref, sems):
    i = pl.program_id(0)
    tm = x2_ref.shape[0]

    def issue(tile):
        slot = tile % 2

        def body(r, carry):
            t = tile * tm + r
            for k, d_ref in enumerate((d0_ref, d1_ref)):
                src = ys_hbm.at[_token_slot(d_ref[t])]
                dst = buf_ref.at[slot, k, _token_slot(r)]
                pltpu.make_async_copy(src, dst, sems.at[slot]).start(priority=k)
            return carry

        lax.fori_loop(0, tm, body, 0, unroll=DMA_UNROLL)

    @pl.when(i == 0)
    def _():
        issue(0)

    @pl.when(i + 1 < pl.num_programs(0))
    def _():
        issue(i + 1)

    slot = i % 2
    for k in range(TOP_K):
        whole = pl.ds(0, tm * TOK_ROWS)
        pltpu.make_async_copy(ys_hbm.at[whole], buf_ref.at[slot, k, whole], sems.at[slot]).wait()
    gt = gt_ref[...]
    g0, g1 = gt[:, 0:1], gt[:, 1:2]
    chunks = zip(_load_token_rows(buf_ref.at[slot, 0], tm), _load_token_rows(buf_ref.at[slot, 1], tm))
    for c, (y0, y1) in enumerate(chunks):
        z_ref[:, c * LANES:(c + 1) * LANES] = y0 * g0 + y1 * g1
    x2 = _layer_norm(DEEPNORM_ALPHA * x1_ref[...] + z_ref[...], g_ref[...], b_ref[...])
    x2_ref[...] = x2
    x2b_ref[...] = x2.astype(BF16)


def _combine_ln(x1, ys, dest0, dest1, gates_t, ln_g, ln_b):
    tm = COMB_TM
    row = lambda width: pl.BlockSpec((tm, width), lambda i, d0, d1: (i, 0))
    vec = pl.BlockSpec((1, D_MODEL), lambda i, d0, d1: (0, 0))
    return pl.pallas_call(
        _combine_ln_kernel,
        out_shape=(jax.ShapeDtypeStruct((N_TOK, D_MODEL), F32),
                   jax.ShapeDtypeStruct((N_TOK, D_MODEL), BF16)),
        grid_spec=pltpu.PrefetchScalarGridSpec(
            num_scalar_prefetch=2,
            grid=(N_TOK // tm,),
            in_specs=[row(D_MODEL), row(TOP_K), pl.BlockSpec(memory_space=pl.ANY), vec, vec],
            out_specs=(row(D_MODEL), row(D_MODEL)),
            scratch_shapes=[pltpu.VMEM((2, TOP_K, tm * TOK_PITCH, LANES), F32),
                            pltpu.VMEM((tm, D_MODEL), F32),
                            pltpu.SemaphoreType.DMA((2,))],
        ),
        compiler_params=_params(1),
        name="combine_ln",
    )(dest0, dest1, x1, gates_t, ys, ln_g.reshape(1, D_MODEL), ln_b.reshape(1, D_MODEL))


def kernel(x, a_w_in, a_conv_w, a_w_out, kv_w, b_w_q, b_w_o, router_w, router_bias,
           moe_w1, moe_w3, moe_w2, ln1_g, ln1_b, ln2_g, ln2_b):
    x = x.reshape(N_TOK, D_MODEL)
    xb = x.astype(BF16)
    rw_pad = jnp.pad(router_w.astype(F32), ((0, 0), (0, LANES - N_EXPERTS)))
    rw_hi = rw_pad.astype(BF16)
    rw_lo = (rw_pad - rw_hi.astype(F32)).astype(BF16)
    rw_split = jnp.concatenate([rw_hi, rw_lo], axis=1)
    rb_col = router_bias.astype(F32).reshape(N_EXPERTS, 1)
    k_tables = _rope_tables(1.0)
    q_tables = _rope_tables(1.0 / math.sqrt(HEAD_DIM))
    dils = [d for _, d in BRANCHES]
    k_sh = v_sh = None
    for i in range(DEPTH):
        if i < N_A_LAYERS:
            y = _conv_proj(xb, a_w_in, a_conv_w, i)
            lhs, w_out = [y], a_w_out[i].astype(BF16)
        else:
            j = i - N_A_LAYERS
            if k_sh is None:
                kv3 = kv_w[None]
                k_sh = [_proj(xb, kv3, 0, g, dils[g], k_tables, rope=True) for g in range(N_BRANCH)]
                v_sh = [_proj(xb, kv3, 0, N_BRANCH + g, dils[g], k_tables, rope=False) for g in range(N_BRANCH)]
            q = [_proj(xb, b_w_q, j, g, dils[g], q_tables, rope=True) for g in range(N_BRANCH)]
            outs = [_attn_branch(q[g], k_sh[g], v_sh[g], dils[g]) for g in range(N_BRANCH)]
            lhs, w_out = [o for o, _ in outs] + [l for _, l in outs], b_w_o[j].astype(BF16)
        x1, xt, eidx, gates = _out_ln_route(lhs, w_out, x, ln1_g[i], ln1_b[i], rw_split, rb_col)
        dest0, dest1, last_blk, nblk_e, sched, gates_t = _build_plan(eidx, gates)
        xs = _moe_dispatch(xt, dest0, dest1, last_blk, nblk_e)
        ys = _moe_ffn(xs, sched, moe_w1, moe_w3, moe_w2, i)
        x, xb = _combine_ln(x1, ys, dest0, dest1, gates_t, ln2_g[i], ln2_b[i])
    return x.reshape(BATCH, SEQ, D_MODEL)
```

```python
import functools
import math

import jax
import jax.numpy as jnp
from jax import lax
from jax.experimental import pallas as pl
from jax.experimental.pallas import tpu as pltpu

D_MODEL = 2048
BATCH = 4
SEQ = 4096
DEPTH = 4
N_TOK = BATCH * SEQ
N_A_LAYERS = DEPTH // 2
CONV_WIDTH = 3
BRANCHES = ((128, 1), (512, 4), (2048, 16))
N_BRANCH = len(BRANCHES)
HEADS_PER_BRANCH = 8
HEAD_DIM = 128
BRANCH_WIDTH = HEADS_PER_BRANCH * HEAD_DIM
SPAN = 128
ROT_DIM = HEAD_DIM // 4
ROPE_THETA = 500000.0
N_EXPERTS = 16
N_GROUPS = 4
EXPERTS_PER_GROUP = N_EXPERTS // N_GROUPS
TOP_K = 2
D_FF = 1408
DEEPNORM_ALPHA = (2.0 * DEPTH) ** 0.25
LN_EPS = 1e-5

F32 = jnp.float32
BF16 = jnp.bfloat16

VMEM_LIMIT_BYTES = 56 * 1024 * 1024
LANES = 128
TOK_ROWS = D_MODEL // LANES
TOK_PITCH = TOK_ROWS + 4
SPLIT_STRIDE = 4

CONV_TM, CONV_TN = 2048, 256
PROJ_TM = 1024
OUT_TM = 256
ATTN_QB = 512
MOE_BLK = 256
MOE_NBLK = (N_TOK * TOP_K) // MOE_BLK + N_EXPERTS
MOE_ROWS = MOE_NBLK * MOE_BLK
MOE_CHUNKS = 8
MOE_UP_ROWS = D_MODEL // MOE_CHUNKS
MOE_DOWN_ROWS = D_FF // MOE_CHUNKS
MOE_STEPS = MOE_NBLK + N_EXPERTS * MOE_CHUNKS
DISPATCH_TM = 512
COMB_TM = 512
DMA_UNROLL = 8


def _params(n_axes):
    return pltpu.CompilerParams(
        dimension_semantics=("arbitrary",) * n_axes,
        vmem_limit_bytes=VMEM_LIMIT_BYTES,
    )


def _store_token_rows(dst_ref, rows, v):
    for c in range(TOK_ROWS):
        dst_ref[pl.ds(c, rows, stride=TOK_PITCH), :] = v[:, c * LANES:(c + 1) * LANES]


def _load_token_rows(src_ref, rows):
    return [src_ref[pl.ds(c, rows, stride=TOK_PITCH), :] for c in range(TOK_ROWS)]


def _token_slot(t):
    return pl.ds(pl.multiple_of(t * TOK_PITCH, 4), TOK_ROWS)


def _conv_proj_kernel(x_ref, wb_ref, wc_ref, wh_ref, cw_ref, y_ref, wbf_ref, ext_ref):
    i = pl.program_id(1)
    tm = x_ref.shape[0]

    @pl.when(i == 0)
    def _():
        wbf_ref[0] = wb_ref[...].astype(BF16)
        wbf_ref[1] = wc_ref[...].astype(BF16)
        wbf_ref[2] = wh_ref[...].astype(BF16)

    @pl.when(i % (SEQ // tm) == 0)
    def _():
        ext_ref[0:8, :] = jnp.zeros((8, ext_ref.shape[1]), F32)

    cw = cw_ref[...]
    x = x_ref[...]
    gate_b = jnp.dot(x, wbf_ref[0], preferred_element_type=F32)
    gate_c = jnp.dot(x, wbf_ref[1], preferred_element_type=F32)
    h = jnp.dot(x, wbf_ref[2], preferred_element_type=F32)
    u = gate_c * h
    ext_ref[8:8 + tm, :] = u
    u1 = ext_ref[7:7 + tm, :]
    u2 = ext_ref[6:6 + tm, :]
    conv = cw[2:3, :] * u + cw[1:2, :] * u1 + cw[0:1, :] * u2
    y_ref[...] = (gate_b * conv).astype(BF16)
    ext_ref[0:8, :] = ext_ref[tm:tm + 8, :]


def _conv_proj(xb, w_in, conv_w, layer):
    tm, tn = CONV_TM, CONV_TN
    nj = D_MODEL // tn
    wspec = lambda off: pl.BlockSpec((None, D_MODEL, tn), lambda j, i: (layer, 0, j + off * nj))
    return pl.pallas_call(
        _conv_proj_kernel,
        out_shape=jax.ShapeDtypeStruct((N_TOK, D_MODEL), BF16),
        grid=(nj, N_TOK // tm),
        in_specs=[
            pl.BlockSpec((tm, D_MODEL), lambda j, i: (i, 0)),
            wspec(0), wspec(1), wspec(2),
            pl.BlockSpec((None, CONV_WIDTH, tn), lambda j, i: (layer, 0, j)),
        ],
        out_specs=pl.BlockSpec((tm, tn), lambda j, i: (i, j)),
        scratch_shapes=[pltpu.VMEM((3, D_MODEL, tn), BF16), pltpu.VMEM((tm + 8, tn), F32)],
        compiler_params=_params(2),
        name="conv_proj",
    )(xb, w_in, w_in, w_in, conv_w)


def _proj_kernel(x_ref, w_ref, c_ref, s1_ref, s2_ref, o_ref, wbf_ref, rot_ref, *maybe_rot2_ref, rope, dilation):
    tm = x_ref.shape[0]
    rot2_ref = maybe_rot2_ref[0] if maybe_rot2_ref else None

    @pl.when(pl.program_id(0) == 0)
    def _():
        wbf_ref[...] = w_ref[...].astype(BF16)

    acc = jnp.dot(x_ref[...], wbf_ref[...], preferred_element_type=F32)
    if rope:
        c = c_ref[...]
        s1 = s1_ref[...]
        s2 = s2_ref[...]
    for hd in range(HEADS_PER_BRANCH):
        sl = slice(hd * HEAD_DIM, (hd + 1) * HEAD_DIM)
        t = acc[:, sl]
        if rope:
            t = t * c + pltpu.roll(t, HEAD_DIM - ROT_DIM // 2, 1) * s1 + pltpu.roll(t, ROT_DIM // 2, 1) * s2
        if dilation == 1:
            o_ref[:, sl] = t.astype(BF16)
            continue
        rot_ref[hd] = t
        for r in range(dilation):
            if dilation <= SPLIT_STRIDE:
                piece = rot_ref.at[hd][pl.ds(r, tm // dilation, stride=dilation), :]
            else:
                q, j = divmod(r, SPLIT_STRIDE)
                if q == 0:
                    rot2_ref[hd, j] = rot_ref.at[hd][pl.ds(j, tm // SPLIT_STRIDE, stride=SPLIT_STRIDE), :]
                piece = rot2_ref.at[hd, j][pl.ds(q, tm // dilation, stride=dilation // SPLIT_STRIDE), :]
            o_ref[:, r * BRANCH_WIDTH + hd * HEAD_DIM:r * BRANCH_WIDTH + (hd + 1) * HEAD_DIM] = piece.astype(BF16)


def _proj(xb, w3d, layer, col, dilation, tables, rope):
    tm = PROJ_TM
    w = BRANCH_WIDTH
    tab = pl.BlockSpec((tm, HEAD_DIM), lambda i: (i % (SEQ // tm), 0))
    scratch = [pltpu.VMEM((D_MODEL, w), BF16), pltpu.VMEM((HEADS_PER_BRANCH, tm, HEAD_DIM), F32)]
    if dilation > SPLIT_STRIDE:
        scratch.append(pltpu.VMEM((HEADS_PER_BRANCH, SPLIT_STRIDE, tm // SPLIT_STRIDE, HEAD_DIM), F32))
    return pl.pallas_call(
        functools.partial(_proj_kernel, rope=rope, dilation=dilation),
        out_shape=jax.ShapeDtypeStruct((N_TOK // dilation, dilation * w), BF16),
        grid=(N_TOK // tm,),
        in_specs=[
            pl.BlockSpec((tm, D_MODEL), lambda i: (i, 0)),
            pl.BlockSpec((None, D_MODEL, w), lambda i: (layer, 0, col)),
            tab, tab, tab,
        ],
        out_specs=pl.BlockSpec((tm // dilation, dilation * w), lambda i: (i, 0)),
        scratch_shapes=scratch,
        compiler_params=_params(1),
        name=("proj_rope" if rope else "proj") + f"_d{dilation}",
    )(xb, w3d, *tables)


def _rope_tables(scale):
    inv_freq = ROPE_THETA ** (-jnp.arange(0, ROT_DIM, 2, dtype=F32) / ROT_DIM)
    ang = jnp.arange(SEQ, dtype=F32)[:, None] * inv_freq[None, :]
    cos, sin = jnp.cos(ang), jnp.sin(ang)
    half = ROT_DIM // 2
    rest = HEAD_DIM - ROT_DIM
    c = jnp.concatenate([cos, cos, jnp.ones((SEQ, rest), F32)], axis=1) * scale
    s1 = jnp.concatenate([-sin, jnp.zeros((SEQ, HEAD_DIM - half), F32)], axis=1) * scale
    s2 = jnp.concatenate([jnp.zeros((SEQ, half), F32), sin, jnp.zeros((SEQ, rest), F32)], axis=1) * scale
    return c, s1, s2


def _attn_kernel(q_ref, kc_ref, kp_ref, vc_ref, vp_ref, o_ref, l_ref):
    qb = q_ref.shape[0]
    qi = lax.broadcasted_iota(jnp.int32, (SPAN, 2 * SPAN), 0)
    kj = lax.broadcasted_iota(jnp.int32, (SPAN, 2 * SPAN), 1)
    dist = SPAN + qi - kj
    band = (dist >= 0) & (dist <= SPAN)
    first_key = jnp.where(pl.program_id(1) == 0, SPAN, 0)
    band_first = band & (kj >= first_key)
    for a in range(qb // SPAN):
        rows = slice(a * SPAN, (a + 1) * SPAN)
        for hd in range(HEADS_PER_BRANCH):
            cols = slice(hd * HEAD_DIM, (hd + 1) * HEAD_DIM)
            q = q_ref[rows, cols]
            if a == 0:
                k_prev, v_prev = kp_ref[:, cols], vp_ref[:, cols]
                mask = band_first
            else:
                prows = slice((a - 1) * SPAN, a * SPAN)
                k_prev, v_prev = kc_ref[prows, cols], vc_ref[prows, cols]
                mask = band
            kk = jnp.concatenate([k_prev, kc_ref[rows, cols]], axis=0)
            vv = jnp.concatenate([v_prev, vc_ref[rows, cols]], axis=0)
            s = lax.dot_general(q, kk, (((1,), (1,)), ((), ())), preferred_element_type=F32)
            s = jnp.where(mask, s, -jnp.inf)
            m = jnp.max(s, axis=-1, keepdims=True)
            p = jnp.exp(s - m)
            den = jnp.sum(p, axis=-1, keepdims=True)
            o = jnp.dot(p.astype(BF16), vv, preferred_element_type=F32) / den
            o_ref[rows, cols] = o.astype(BF16)
            l_ref[rows, cols] = jnp.broadcast_to(m + jnp.log(den), (SPAN, HEAD_DIM))


def _attn_branch(q, k, v, dilation):
    sub = SEQ // dilation
    qb = min(ATTN_QB, sub)
    w = BRANCH_WIDTH
    view = lambda t: t.reshape(BATCH, sub, dilation * w)
    cur = pl.BlockSpec((None, qb, w), lambda b, n, r: (b, n, r))
    prev = pl.BlockSpec((None, SPAN, w),
                        lambda b, n, r: (b, jnp.maximum(n * (qb // SPAN) - 1, 0), r))
    o, l = pl.pallas_call(
        _attn_kernel,
        out_shape=(jax.ShapeDtypeStruct((BATCH, sub, dilation * w), BF16),
                   jax.ShapeDtypeStruct((BATCH, sub, dilation * w), F32)),
        grid=(BATCH, sub // qb, dilation),
        in_specs=[cur, cur, prev, cur, prev],
        out_specs=(cur, cur),
        compiler_params=_params(3),
        name=f"dilated_attn_d{dilation}",
    )(view(q), view(k), view(k), view(v), view(v))
    return o.reshape(N_TOK // dilation, dilation * w), l.reshape(N_TOK // dilation, dilation * w)


def _layer_norm(z, g, b):
    mean = jnp.mean(z, axis=-1, keepdims=True)
    zc = z - mean
    var = jnp.mean(zc * zc, axis=-1, keepdims=True)
    return zc * lax.rsqrt(var + LN_EPS) * g + b


def _route(logits_t, bias_col):
    scores = jax.nn.sigmoid(logits_t)
    sel = scores + bias_col
    sel_r = [sel[e:e + 1, :] for e in range(N_EXPERTS)]
    sc_r = [scores[e:e + 1, :] for e in range(N_EXPERTS)]
    best_g = None
    for g in range(N_GROUPS):
        v = sel_r[g * EXPERTS_PER_GROUP:(g + 1) * EXPERTS_PER_GROUP]
        gs = None
        for a in range(EXPERTS_PER_GROUP):
            for b in range(a + 1, EXPERTS_PER_GROUP):
                ps = v[a] + v[b]
                gs = ps if gs is None else jnp.maximum(gs, ps)
        if best_g is None:
            best_g, gidx = gs, jnp.zeros_like(gs, dtype=jnp.int32)
        else:
            upd = gs > best_g
            best_g = jnp.where(upd, gs, best_g)
            gidx = jnp.where(upd, g, gidx)
    cand, raw = [], []
    for j in range(EXPERTS_PER_GROUP):
        cv, rv = sel_r[j], sc_r[j]
        for g in range(1, N_GROUPS):
            pick = gidx == g
            cv = jnp.where(pick, sel_r[g * EXPERTS_PER_GROUP + j], cv)
            rv = jnp.where(pick, sc_r[g * EXPERTS_PER_GROUP + j], rv)
        cand.append(cv)
        raw.append(rv)

    def argmax4(vals):
        best, idx = vals[0], jnp.zeros_like(gidx)
        for j in range(1, EXPERTS_PER_GROUP):
            upd = vals[j] > best
            best = jnp.where(upd, vals[j], best)
            idx = jnp.where(upd, j, idx)
        return idx

    i1 = argmax4(cand)
    i2 = argmax4([jnp.where(i1 == j, -jnp.inf, cand[j]) for j in range(EXPERTS_PER_GROUP)])
    pick_raw = lambda idx: sum(jnp.where(idx == j, raw[j], 0.0) for j in range(EXPERTS_PER_GROUP))
    g1, g2 = pick_raw(i1), pick_raw(i2)
    tot = g1 + g2
    e1 = gidx * EXPERTS_PER_GROUP + i1
    e2 = gidx * EXPERTS_PER_GROUP + i2
    return e1, e2, g1 / tot, g2 / tot


def _out_ln_route_tail(mix, x_ref, g_ref, b_ref, rw_ref, rb_ref, x1_ref, e_ref, gt_ref):
    z = DEEPNORM_ALPHA * x_ref[...] + mix
    x1 = _layer_norm(z, g_ref[...], b_ref[...])
    x1_ref[...] = x1
    x_hi = x1.astype(BF16)
    x_lo = (x1 - x_hi.astype(F32)).astype(BF16)
    p_hi = jnp.dot(x_hi, rw_ref[...], preferred_element_type=F32)
    p_lo = jnp.dot(x_lo, rw_ref[:, :LANES], preferred_element_type=F32)
    logits = p_hi[:, :LANES] + (p_hi[:, LANES:] + p_lo)
    logits_t = logits.T[0:N_EXPERTS, :]
    e1, e2, g1, g2 = _route(logits_t, rb_ref[...])
    e_ref[0:1, :] = e1
    e_ref[1:2, :] = e2
    gt_ref[0:1, :] = g1
    gt_ref[1:2, :] = g2


def _out_ln_route_a_kernel(a_ref, w_ref, *rest):
    mix = jnp.dot(a_ref[...], w_ref[...], preferred_element_type=F32)
    _out_ln_route_tail(mix, *rest)


def _out_ln_route_b_kernel(o0_ref, o1_ref, o2_ref, l0_ref, l1_ref, l2_ref, w_ref,
                           x_ref, g_ref, b_ref, rw_ref, rb_ref, x1_ref, e_ref, gt_ref,
                           os_ref, ls_ref):
    tm = x_ref.shape[0]
    merged = []
    for hd in range(HEADS_PER_BRANCH):
        for g, (o_ref, l_ref) in enumerate(((o1_ref, l1_ref), (o2_ref, l2_ref))):
            d = BRANCHES[g + 1][1]
            for r in range(d):
                cols = slice(r * BRANCH_WIDTH + hd * HEAD_DIM, r * BRANCH_WIDTH + (hd + 1) * HEAD_DIM)
                os_ref.at[g, hd][pl.ds(r, tm // d, stride=d), :] = o_ref[:, cols].astype(F32)
                ls_ref.at[g, hd][pl.ds(r, tm // d, stride=d), :] = l_ref[:, cols]
        cols = slice(hd * HEAD_DIM, (hd + 1) * HEAD_DIM)
        l0, l1, l2 = l0_ref[:, cols], ls_ref[0, hd], ls_ref[1, hd]
        lm = jnp.maximum(jnp.maximum(l0, l1), l2)
        w0, w1, w2 = jnp.exp(l0 - lm), jnp.exp(l1 - lm), jnp.exp(l2 - lm)
        m = (o0_ref[:, cols].astype(F32) * w0 + os_ref[0, hd] * w1 + os_ref[1, hd] * w2) / (w0 + w1 + w2)
        merged.append(m.astype(BF16))
    mix = jnp.dot(jnp.concatenate(merged, axis=1), w_ref[...], preferred_element_type=F32)
    _out_ln_route_tail(mix, x_ref, g_ref, b_ref, rw_ref, rb_ref, x1_ref, e_ref, gt_ref)


def _out_ln_route(lhs, w_bf, x, ln_g, ln_b, rw_split, rb_col):
    tm = OUT_TM
    kdim = w_bf.shape[0]
    row = lambda width: pl.BlockSpec((tm, width), lambda i: (i, 0))
    full = lambda shape: pl.BlockSpec(shape, lambda i: (0,) * len(shape))
    lane_row = pl.BlockSpec((TOP_K, tm), lambda i: (0, i))
    if len(lhs) == 1:
        body, name, lhs_specs, scratch = _out_ln_route_a_kernel, "out_ln_route_conv", [row(kdim)], []
    else:
        body, name = _out_ln_route_b_kernel, "out_ln_route_attn"
        blocked = [pl.BlockSpec((tm // d, d * BRANCH_WIDTH), lambda i: (i, 0)) for _, d in BRANCHES]
        lhs_specs = blocked + blocked
        scratch = [pltpu.VMEM((N_BRANCH - 1, HEADS_PER_BRANCH, tm, HEAD_DIM), F32)] * 2
    return pl.pallas_call(
        body,
        out_shape=(jax.ShapeDtypeStruct((N_TOK, D_MODEL), F32),
                   jax.ShapeDtypeStruct((TOP_K, N_TOK), jnp.int32),
                   jax.ShapeDtypeStruct((TOP_K, N_TOK), F32)),
        grid=(N_TOK // tm,),
        in_specs=lhs_specs + [
            full((kdim, D_MODEL)), row(D_MODEL), full((1, D_MODEL)), full((1, D_MODEL)),
            full((D_MODEL, 2 * LANES)), full((N_EXPERTS, 1)),
        ],
        out_specs=(row(D_MODEL), lane_row, lane_row),
        scratch_shapes=scratch,
        compiler_params=_params(1),
        name=name,
    )(*lhs, w_bf, x, ln_g.reshape(1, D_MODEL), ln_b.reshape(1, D_MODEL), rw_split, rb_col)


def _build_plan(eidx, gates):
    flat_e = eidx.reshape(-1)
    onehot = (flat_e[:, None] == jnp.arange(N_EXPERTS, dtype=jnp.int32)[None, :]).astype(jnp.int32)
    csum = jnp.cumsum(onehot, axis=0)
    rank = jnp.sum(onehot * csum, axis=1) - 1
    counts = csum[-1]
    nblk_e = ((counts + MOE_BLK - 1) // MOE_BLK).astype(jnp.int32)
    blk_end = jnp.cumsum(nblk_e).astype(jnp.int32)
    blk_start = blk_end - nblk_e
    dest = (blk_start[flat_e] * MOE_BLK + rank).astype(jnp.int32)
    n_valid = blk_end[-1]

    chunks = jnp.full((1,), MOE_CHUNKS, jnp.int32)
    phase_len = jnp.concatenate([chunks, jnp.maximum(nblk_e[:-1], MOE_CHUNKS), nblk_e[-1:]])
    phase_end = jnp.cumsum(phase_len).astype(jnp.int32)
    phase_start = phase_end - phase_len
    step = jnp.arange(MOE_STEPS, dtype=jnp.int32)
    ph = jnp.minimum(jnp.sum((step[:, None] >= phase_end[None, :]).astype(jnp.int32), axis=1), N_EXPERTS)
    off = step - phase_start[ph]
    live = step < phase_end[-1]
    ce = jnp.clip(ph - 1, 0, N_EXPERTS - 1)
    nb_c = jnp.where(ph >= 1, nblk_e[ce], 0)
    comp_on = (live & (ph >= 1) & (off < nb_c)).astype(jnp.int32)
    comp_blk = jnp.clip(jnp.where(ph >= 1, blk_start[ce], 0) + jnp.minimum(off, nb_c - 1), 0, n_valid - 1)
    load_on = (live & (ph < N_EXPERTS) & (off < MOE_CHUNKS)).astype(jnp.int32)
    load_e = jnp.minimum(ph, N_EXPERTS - 1)
    load_c = jnp.where(ph < N_EXPERTS, jnp.minimum(off, MOE_CHUNKS - 1), MOE_CHUNKS - 1)
    sched = (comp_on, comp_blk.astype(jnp.int32), (ph + 1) % 2, load_on, load_e, load_c.astype(jnp.int32), ph % 2)
    return dest[:N_TOK], dest[N_TOK:], blk_end - 1, nblk_e, sched, gates.T


def _dispatch_kernel(d0_ref, d1_ref, last_ref, nblk_ref, x1_ref, xs_hbm, xp_ref, zero_ref, zsem, sem):
    i = pl.program_id(0)
    tm = x1_ref.shape[0]
    blk_rows = MOE_BLK * TOK_PITCH
    _store_token_rows(xp_ref, tm, x1_ref[...])

    @pl.when(i == 0)
    def _():
        zero_ref[...] = jnp.zeros_like(zero_ref)

        def zero_copy(e):
            start = pl.multiple_of(last_ref[e] * blk_rows, blk_rows)
            return pltpu.make_async_copy(zero_ref, xs_hbm.at[pl.ds(start, blk_rows)], zsem)

        for e in range(N_EXPERTS):
            @pl.when(nblk_ref[e] > 0)
            def _():
                zero_copy(e).start()
        for e in range(N_EXPERTS):
            @pl.when(nblk_ref[e] > 0)
            def _():
                zero_copy(e).wait()

    def body(r, carry):
        t = i * tm + r
        src = xp_ref.at[_token_slot(r)]
        for k, d_ref in enumerate((d0_ref, d1_ref)):
            dst = xs_hbm.at[_token_slot(d_ref[t])]
            pltpu.make_async_copy(src, dst, sem).start(priority=k)
        return carry

    lax.fori_loop(0, tm, body, 0, unroll=DMA_UNROLL)
    for _ in range(TOP_K):
        whole = pl.ds(0, tm * TOK_ROWS)
        pltpu.make_async_copy(xp_ref.at[whole], xs_hbm.at[whole], sem).wait()


def _moe_dispatch(xp, dest0, dest1, last_blk, nblk_e):
    tm = DISPATCH_TM
    return pl.pallas_call(
        _dispatch_kernel,
        out_shape=jax.ShapeDtypeStruct((MOE_ROWS * TOK_PITCH, LANES), F32),
        grid_spec=pltpu.PrefetchScalarGridSpec(
            num_scalar_prefetch=4,
            grid=(N_TOK // tm,),
            in_specs=[pl.BlockSpec((tm, D_MODEL), lambda i, *_: (i, 0))],
            out_specs=pl.BlockSpec(memory_space=pl.ANY),
            scratch_shapes=[pltpu.VMEM((tm * TOK_PITCH, LANES), F32),
                            pltpu.VMEM((MOE_BLK * TOK_PITCH, LANES), F32),
                            pltpu.SemaphoreType.DMA, pltpu.SemaphoreType.DMA],
        ),
        compiler_params=_params(1),
        name="moe_dispatch",
    )(dest0, dest1, last_blk, nblk_e, xp)


def _moe_ffn_kernel(con_ref, cblk_ref, cslot_ref, lon_ref, le_ref, lc_ref, lslot_ref,
                    xs_ref, w1_ref, w3_ref, w2_ref, ys_ref, w13_ref, w2b_ref):
    s = pl.program_id(0)
    blk = MOE_BLK
    n_ff = D_FF // LANES

    @pl.when(lon_ref[s] == 1)
    def _():
        slot = lslot_ref[s]
        rows = pl.ds(pl.multiple_of(lc_ref[s] * MOE_UP_ROWS, MOE_UP_ROWS), MOE_UP_ROWS)
        for j in range(n_ff):
            src = slice(j * LANES, (j + 1) * LANES)
            w13_ref[slot, rows, 2 * j * LANES:(2 * j + 1) * LANES] = w1_ref[:, src].astype(BF16)
            w13_ref[slot, rows, (2 * j + 1) * LANES:(2 * j + 2) * LANES] = w3_ref[:, src].astype(BF16)
        rows2 = pl.ds(pl.multiple_of(lc_ref[s] * MOE_DOWN_ROWS, 16), MOE_DOWN_ROWS)
        w2b_ref[slot, rows2, :] = w2_ref[...].astype(BF16)

    @pl.when(con_ref[s] == 1)
    def _():
        slot = cslot_ref[s]
        x = jnp.concatenate([p.astype(BF16) for p in _load_token_rows(xs_ref, blk)], axis=1)
        g = []
        for q in range(0, n_ff, 2):
            npair = min(2, n_ff - q)
            h = jnp.dot(x, w13_ref[slot, :, 2 * q * LANES:2 * (q + npair) * LANES], preferred_element_type=F32)
            for k in range(npair):
                h1, h3 = h[:, 2 * k * LANES:(2 * k + 1) * LANES], h[:, (2 * k + 1) * LANES:(2 * k + 2) * LANES]
                g.append((jax.nn.silu(h1) * h3).astype(BF16))
        y = jnp.dot(jnp.concatenate(g, axis=1), w2b_ref[slot], preferred_element_type=F32)
        _store_token_rows(ys_ref, blk, y)


def _moe_ffn(xs, sched, w1, w3, w2, layer):
    blk = MOE_BLK
    rows = pl.BlockSpec((blk * TOK_PITCH, LANES), lambda s, con, cblk, *_: (cblk[s], 0))
    up = pl.BlockSpec((None, None, MOE_UP_ROWS, D_FF),
                      lambda s, con, cblk, cslot, lon, le, lc, *_: (layer, le[s], lc[s], 0))
    down = pl.BlockSpec((None, None, MOE_DOWN_ROWS, D_MODEL),
                        lambda s, con, cblk, cslot, lon, le, lc, *_: (layer, le[s], lc[s], 0))
    return pl.pallas_call(
        _moe_ffn_kernel,
        out_shape=jax.ShapeDtypeStruct((MOE_ROWS * TOK_PITCH, LANES), F32),
        grid_spec=pltpu.PrefetchScalarGridSpec(
            num_scalar_prefetch=len(sched),
            grid=(MOE_STEPS,),
            in_specs=[rows, up, up, down],
            out_specs=rows,
            scratch_shapes=[pltpu.VMEM((2, D_MODEL, 2 * D_FF), BF16), pltpu.VMEM((2, D_FF, D_MODEL), BF16)],
        ),
        compiler_params=_params(1),
        name="moe_ffn",
    )(*sched, xs, w1, w3, w2)


def _combine_ln_kernel(d0_ref, d1_ref, x1_ref, gt_ref, ys_hbm, g_ref, b_ref, x2_ref, x2b_ref,
                       buf_ref, z_ref, sems):
    i = pl.program_id(0)
    tm = x2_ref.shape[0]

    def issue(tile):
        slot = tile % 2

        def body(r, carry):
            t = tile * tm + r
            for k, d_ref in enumerate((d0_ref, d1_ref)):
                src = ys_hbm.at[_token_slot(d_ref[t])]
                dst = buf_ref.at[slot, k, _token_slot(r)]
                pltpu.make_async_copy(src, dst, sems.at[slot]).start(priority=k)
            return carry

        lax.fori_loop(0, tm, body, 0, unroll=DMA_UNROLL)

    @pl.when(i == 0)
    def _():
        issue(0)

    @pl.when(i + 1 < pl.num_programs(0))
    def _():
        issue(i + 1)

    slot = i % 2
    for k in range(TOP_K):
        whole = pl.ds(0, tm * TOK_ROWS)
        pltpu.make_async_copy(ys_hbm.at[whole], buf_ref.at[slot, k, whole], sems.at[slot]).wait()
    gt = gt_ref[...]
    g0, g1 = gt[:, 0:1], gt[:, 1:2]
    chunks = zip(_load_token_rows(buf_ref.at[slot, 0], tm), _load_token_rows(buf_ref.at[slot, 1], tm))
    for c, (y0, y1) in enumerate(chunks):
        z_ref[:, c * LANES:(c + 1) * LANES] = y0 * g0 + y1 * g1
    x2 = _layer_norm(DEEPNORM_ALPHA * x1_ref[...] + z_ref[...], g_ref[...], b_ref[...])
    x2_ref[...] = x2
    x2b_ref[...] = x2.astype(BF16)


def _combine_ln(x1, ys, dest0, dest1, gates_t, ln_g, ln_b):
    tm = COMB_TM
    row = lambda width: pl.BlockSpec((tm, width), lambda i, d0, d1: (i, 0))
    vec = pl.BlockSpec((1, D_MODEL), lambda i, d0, d1: (0, 0))
    return pl.pallas_call(
        _combine_ln_kernel,
        out_shape=(jax.ShapeDtypeStruct((N_TOK, D_MODEL), F32),
                   jax.ShapeDtypeStruct((N_TOK, D_MODEL), BF16)),
        grid_spec=pltpu.PrefetchScalarGridSpec(
            num_scalar_prefetch=2,
            grid=(N_TOK // tm,),
            in_specs=[row(D_MODEL), row(TOP_K), pl.BlockSpec(memory_space=pl.ANY), vec, vec],
            out_specs=(row(D_MODEL), row(D_MODEL)),
            scratch_shapes=[pltpu.VMEM((2, TOP_K, tm * TOK_PITCH, LANES), F32),
                            pltpu.VMEM((tm, D_MODEL), F32),
                            pltpu.SemaphoreType.DMA((2,))],
        ),
        compiler_params=_params(1),
        name="combine_ln",
    )(dest0, dest1, x1, gates_t, ys, ln_g.reshape(1, D_MODEL), ln_b.reshape(1, D_MODEL))


def kernel(x, a_w_in, a_conv_w, a_w_out, kv_w, b_w_q, b_w_o, router_w, router_bias,
           moe_w1, moe_w3, moe_w2, ln1_g, ln1_b, ln2_g, ln2_b):
    x = x.reshape(N_TOK, D_MODEL)
    xb = x.astype(BF16)
    rw_pad = jnp.pad(router_w.astype(F32), ((0, 0), (0, LANES - N_EXPERTS)))
    rw_hi = rw_pad.astype(BF16)
    rw_lo = (rw_pad - rw_hi.astype(F32)).astype(BF16)
    rw_split = jnp.concatenate([rw_hi, rw_lo], axis=1)
    rb_col = router_bias.astype(F32).reshape(N_EXPERTS, 1)
    k_tables = _rope_tables(1.0)
    q_tables = _rope_tables(1.0 / math.sqrt(HEAD_DIM))
    dils = [d for _, d in BRANCHES]
    k_sh = v_sh = None
    for i in range(DEPTH):
        if i < N_A_LAYERS:
            y = _conv_proj(xb, a_w_in, a_conv_w, i)
            lhs, w_out = [y], a_w_out[i].astype(BF16)
        else:
            j = i - N_A_LAYERS
            if k_sh is None:
                kv3 = kv_w[None]
                k_sh = [_proj(xb, kv3, 0, g, dils[g], k_tables, rope=True) for g in range(N_BRANCH)]
                v_sh = [_proj(xb, kv3, 0, N_BRANCH + g, dils[g], k_tables, rope=False) for g in range(N_BRANCH)]
            q = [_proj(xb, b_w_q, j, g, dils[g], q_tables, rope=True) for g in range(N_BRANCH)]
            outs = [_attn_branch(q[g], k_sh[g], v_sh[g], dils[g]) for g in range(N_BRANCH)]
            lhs, w_out = [o for o, _ in outs] + [l for _, l in outs], b_w_o[j].astype(BF16)
        x1, eidx, gates = _out_ln_route(lhs, w_out, x, ln1_g[i], ln1_b[i], rw_split, rb_col)
        dest0, dest1, last_blk, nblk_e, sched, gates_t = _build_plan(eidx, gates)
        xs = _moe_dispatch(x1, dest0, dest1, last_blk, nblk_e)
        ys = _moe_ffn(xs, sched, moe_w1, moe_w3, moe_w2, i)
        x, xb = _combine_ln(x1, ys, dest0, dest1, gates_t, ln2_g[i], ln2_b[i])
    return x.reshape(BATCH, SEQ, D_MODEL)
```

```python
import functools
import math

import jax
import jax.numpy as jnp
from jax import lax
from jax.experimental import pallas as pl
from jax.experimental.pallas import tpu as pltpu

D_MODEL = 2048
BATCH = 4
SEQ = 4096
DEPTH = 4
N_TOK = BATCH * SEQ
N_A_LAYERS = DEPTH // 2
CONV_WIDTH = 3
BRANCHES = ((128, 1), (512, 4), (2048, 16))
N_BRANCH = len(BRANCHES)
HEADS_PER_BRANCH = 8
HEAD_DIM = 128
BRANCH_WIDTH = HEADS_PER_BRANCH * HEAD_DIM
SPAN = 128
ROT_DIM = HEAD_DIM // 4
ROPE_THETA = 500000.0
N_EXPERTS = 16
N_GROUPS = 4
EXPERTS_PER_GROUP = N_EXPERTS // N_GROUPS
TOP_K = 2
D_FF = 1408
DEEPNORM_ALPHA = (2.0 * DEPTH) ** 0.25
LN_EPS = 1e-5

F32 = jnp.float32
BF16 = jnp.bfloat16

VMEM_LIMIT_BYTES = 56 * 1024 * 1024
LANES = 128
TOK_ROWS = D_MODEL // LANES
TOK_PITCH = TOK_ROWS + 4
SPLIT_STRIDE = 4

CONV_TM, CONV_TN = 2048, 256
PROJ_TM = 1024
OUT_TM = 256
ATTN_QB = 1024
MOE_BLK = 256
MOE_NBLK = (N_TOK * TOP_K) // MOE_BLK + N_EXPERTS
MOE_ROWS = MOE_NBLK * MOE_BLK
MOE_CHUNKS = 8
MOE_UP_ROWS = D_MODEL // MOE_CHUNKS
MOE_DOWN_ROWS = D_FF // MOE_CHUNKS
MOE_STEPS = MOE_NBLK + N_EXPERTS * MOE_CHUNKS
DISPATCH_TM = 512
COMB_TM = 512
DMA_UNROLL = 8


def _params(n_axes):
    return pltpu.CompilerParams(
        dimension_semantics=("arbitrary",) * n_axes,
        vmem_limit_bytes=VMEM_LIMIT_BYTES,
    )


def _store_token_rows(dst_ref, rows, v):
    for c in range(TOK_ROWS):
        dst_ref[pl.ds(c, rows, stride=TOK_PITCH), :] = v[:, c * LANES:(c + 1) * LANES]


def _load_token_rows(src_ref, rows):
    return [src_ref[pl.ds(c, rows, stride=TOK_PITCH), :] for c in range(TOK_ROWS)]


def _token_slot(t):
    return pl.ds(pl.multiple_of(t * TOK_PITCH, 4), TOK_ROWS)


def _conv_proj_kernel(x_ref, wb_ref, wc_ref, wh_ref, cw_ref, y_ref, wbf_ref, ext_ref):
    i = pl.program_id(1)
    tm = x_ref.shape[0]

    @pl.when(i == 0)
    def _():
        wbf_ref[0] = wb_ref[...].astype(BF16)
        wbf_ref[1] = wc_ref[...].astype(BF16)
        wbf_ref[2] = wh_ref[...].astype(BF16)

    @pl.when(i % (SEQ // tm) == 0)
    def _():
        ext_ref[0:8, :] = jnp.zeros((8, ext_ref.shape[1]), F32)

    cw = cw_ref[...]
    x = x_ref[...]
    gate_b = jnp.dot(x, wbf_ref[0], preferred_element_type=F32)
    gate_c = jnp.dot(x, wbf_ref[1], preferred_element_type=F32)
    h = jnp.dot(x, wbf_ref[2], preferred_element_type=F32)
    u = gate_c * h
    ext_ref[8:8 + tm, :] = u
    u1 = ext_ref[7:7 + tm, :]
    u2 = ext_ref[6:6 + tm, :]
    conv = cw[2:3, :] * u + cw[1:2, :] * u1 + cw[0:1, :] * u2
    y_ref[...] = (gate_b * conv).astype(BF16)
    ext_ref[0:8, :] = ext_ref[tm:tm + 8, :]


def _conv_proj(xb, w_in, conv_w, layer):
    tm, tn = CONV_TM, CONV_TN
    nj = D_MODEL // tn
    wspec = lambda off: pl.BlockSpec((None, D_MODEL, tn), lambda j, i: (layer, 0, j + off * nj))
    return pl.pallas_call(
        _conv_proj_kernel,
        out_shape=jax.ShapeDtypeStruct((N_TOK, D_MODEL), BF16),
        grid=(nj, N_TOK // tm),
        in_specs=[
            pl.BlockSpec((tm, D_MODEL), lambda j, i: (i, 0)),
            wspec(0), wspec(1), wspec(2),
            pl.BlockSpec((None, CONV_WIDTH, tn), lambda j, i: (layer, 0, j)),
        ],
        out_specs=pl.BlockSpec((tm, tn), lambda j, i: (i, j)),
        scratch_shapes=[pltpu.VMEM((3, D_MODEL, tn), BF16), pltpu.VMEM((tm + 8, tn), F32)],
        compiler_params=_params(2),
        name="conv_proj",
    )(xb, w_in, w_in, w_in, conv_w)


def _proj_kernel(x_ref, w_ref, c_ref, s1_ref, s2_ref, o_ref, wbf_ref, rot_ref, *maybe_rot2_ref, rope, dilation):
    tm = x_ref.shape[0]
    rot2_ref = maybe_rot2_ref[0] if maybe_rot2_ref else None

    @pl.when(pl.program_id(0) == 0)
    def _():
        wbf_ref[...] = w_ref[...].astype(BF16)

    acc = jnp.dot(x_ref[...], wbf_ref[...], preferred_element_type=F32)
    if rope:
        c = c_ref[...]
        s1 = s1_ref[...]
        s2 = s2_ref[...]
    for hd in range(HEADS_PER_BRANCH):
        sl = slice(hd * HEAD_DIM, (hd + 1) * HEAD_DIM)
        t = acc[:, sl]
        if rope:
            t = t * c + pltpu.roll(t, HEAD_DIM - ROT_DIM // 2, 1) * s1 + pltpu.roll(t, ROT_DIM // 2, 1) * s2
        if dilation == 1:
            o_ref[:, sl] = t.astype(BF16)
            continue
        rot_ref[hd] = t
        for r in range(dilation):
            if dilation <= SPLIT_STRIDE:
                piece = rot_ref.at[hd][pl.ds(r, tm // dilation, stride=dilation), :]
            else:
                q, j = divmod(r, SPLIT_STRIDE)
                if q == 0:
                    rot2_ref[hd, j] = rot_ref.at[hd][pl.ds(j, tm // SPLIT_STRIDE, stride=SPLIT_STRIDE), :]
                piece = rot2_ref.at[hd, j][pl.ds(q, tm // dilation, stride=dilation // SPLIT_STRIDE), :]
            o_ref[:, r * BRANCH_WIDTH + hd * HEAD_DIM:r * BRANCH_WIDTH + (hd + 1) * HEAD_DIM] = piece.astype(BF16)


def _proj(xb, w3d, layer, col, dilation, tables, rope):
    tm = PROJ_TM
    w = BRANCH_WIDTH
    tab = pl.BlockSpec((tm, HEAD_DIM), lambda i: (i % (SEQ // tm), 0))
    scratch = [pltpu.VMEM((D_MODEL, w), BF16), pltpu.VMEM((HEADS_PER_BRANCH, tm, HEAD_DIM), F32)]
    if dilation > SPLIT_STRIDE:
        scratch.append(pltpu.VMEM((HEADS_PER_BRANCH, SPLIT_STRIDE, tm // SPLIT_STRIDE, HEAD_DIM), F32))
    return pl.pallas_call(
        functools.partial(_proj_kernel, rope=rope, dilation=dilation),
        out_shape=jax.ShapeDtypeStruct((N_TOK // dilation, dilation * w), BF16),
        grid=(N_TOK // tm,),
        in_specs=[
            pl.BlockSpec((tm, D_MODEL), lambda i: (i, 0)),
            pl.BlockSpec((None, D_MODEL, w), lambda i: (layer, 0, col)),
            tab, tab, tab,
        ],
        out_specs=pl.BlockSpec((tm // dilation, dilation * w), lambda i: (i, 0)),
        scratch_shapes=scratch,
        compiler_params=_params(1),
        name=("proj_rope" if rope else "proj") + f"_d{dilation}",
    )(xb, w3d, *tables)


def _rope_tables(scale):
    inv_freq = ROPE_THETA ** (-jnp.arange(0, ROT_DIM, 2, dtype=F32) / ROT_DIM)
    ang = jnp.arange(SEQ, dtype=F32)[:, None] * inv_freq[None, :]
    cos, sin = jnp.cos(ang), jnp.sin(ang)
    half = ROT_DIM // 2
    rest = HEAD_DIM - ROT_DIM
    c = jnp.concatenate([cos, cos, jnp.ones((SEQ, rest), F32)], axis=1) * scale
    s1 = jnp.concatenate([-sin, jnp.zeros((SEQ, HEAD_DIM - half), F32)], axis=1) * scale
    s2 = jnp.concatenate([jnp.zeros((SEQ, half), F32), sin, jnp.zeros((SEQ, rest), F32)], axis=1) * scale
    return c, s1, s2


def _attn_kernel(q_ref, kc_ref, kp_ref, vc_ref, vp_ref, o_ref, l_ref):
    qb = q_ref.shape[0]
    qi = lax.broadcasted_iota(jnp.int32, (SPAN, 2 * SPAN), 0)
    kj = lax.broadcasted_iota(jnp.int32, (SPAN, 2 * SPAN), 1)
    dist = SPAN + qi - kj
    band = (dist >= 0) & (dist <= SPAN)
    first_key = jnp.where(pl.program_id(1) == 0, SPAN, 0)
    band_first = band & (kj >= first_key)
    for a in range(qb // SPAN):
        rows = slice(a * SPAN, (a + 1) * SPAN)
        for hd in range(q_ref.shape[1] // HEAD_DIM):
            cols = slice(hd * HEAD_DIM, (hd + 1) * HEAD_DIM)
            q = q_ref[rows, cols]
            if a == 0:
                k_prev, v_prev = kp_ref[:, cols], vp_ref[:, cols]
                mask = band_first
            else:
                prows = slice((a - 1) * SPAN, a * SPAN)
                k_prev, v_prev = kc_ref[prows, cols], vc_ref[prows, cols]
                mask = band
            kk = jnp.concatenate([k_prev, kc_ref[rows, cols]], axis=0)
            vv = jnp.concatenate([v_prev, vc_ref[rows, cols]], axis=0)
            s = lax.dot_general(q, kk, (((1,), (1,)), ((), ())), preferred_element_type=F32)
            s = jnp.where(mask, s, -jnp.inf)
            m = jnp.max(s, axis=-1, keepdims=True)
            p = jnp.exp(s - m)
            den = jnp.sum(p, axis=-1, keepdims=True)
            o = jnp.dot(p.astype(BF16), vv, preferred_element_type=F32) / den
            o_ref[rows, cols] = o.astype(BF16)
            l_ref[rows, cols] = jnp.broadcast_to(m + jnp.log(den), (SPAN, HEAD_DIM))


def _attn_branch(q, k, v, dilation):
    sub = SEQ // dilation
    qb = min(ATTN_QB, sub)
    n_res = min(dilation, ATTN_QB // qb)
    w = n_res * BRANCH_WIDTH
    view = lambda t: t.reshape(BATCH, sub, dilation * BRANCH_WIDTH)
    cur = pl.BlockSpec((None, qb, w), lambda b, n, r: (b, n, r))
    prev = pl.BlockSpec((None, SPAN, w),
                        lambda b, n, r: (b, jnp.maximum(n * (qb // SPAN) - 1, 0), r))
    o, l = pl.pallas_call(
        _attn_kernel,
        out_shape=(jax.ShapeDtypeStruct((BATCH, sub, dilation * BRANCH_WIDTH), BF16),
                   jax.ShapeDtypeStruct((BATCH, sub, dilation * BRANCH_WIDTH), F32)),
        grid=(BATCH, sub // qb, dilation // n_res),
        in_specs=[cur, cur, prev, cur, prev],
        out_specs=(cur, cur),
        compiler_params=_params(3),
        name=f"dilated_attn_d{dilation}",
    )(view(q), view(k), view(k), view(v), view(v))
    width = dilation * BRANCH_WIDTH
    return o.reshape(N_TOK // dilation, width), l.reshape(N_TOK // dilation, width)


def _layer_norm(z, g, b):
    mean = jnp.mean(z, axis=-1, keepdims=True)
    zc = z - mean
    var = jnp.mean(zc * zc, axis=-1, keepdims=True)
    return zc * lax.rsqrt(var + LN_EPS) * g + b


def _route(logits_t, bias_col):
    scores = jax.nn.sigmoid(logits_t)
    sel = scores + bias_col
    sel_r = [sel[e:e + 1, :] for e in range(N_EXPERTS)]
    sc_r = [scores[e:e + 1, :] for e in range(N_EXPERTS)]
    best_g = None
    for g in range(N_GROUPS):
        v = sel_r[g * EXPERTS_PER_GROUP:(g + 1) * EXPERTS_PER_GROUP]
        gs = None
        for a in range(EXPERTS_PER_GROUP):
            for b in range(a + 1, EXPERTS_PER_GROUP):
                ps = v[a] + v[b]
                gs = ps if gs is None else jnp.maximum(gs, ps)
        if best_g is None:
            best_g, gidx = gs, jnp.zeros_like(gs, dtype=jnp.int32)
        else:
            upd = gs > best_g
            best_g = jnp.where(upd, gs, best_g)
            gidx = jnp.where(upd, g, gidx)
    cand, raw = [], []
    for j in range(EXPERTS_PER_GROUP):
        cv, rv = sel_r[j], sc_r[j]
        for g in range(1, N_GROUPS):
            pick = gidx == g
            cv = jnp.where(pick, sel_r[g * EXPERTS_PER_GROUP + j], cv)
            rv = jnp.where(pick, sc_r[g * EXPERTS_PER_GROUP + j], rv)
        cand.append(cv)
        raw.append(rv)

    def argmax4(vals):
        best, idx = vals[0], jnp.zeros_like(gidx)
        for j in range(1, EXPERTS_PER_GROUP):
            upd = vals[j] > best
            best = jnp.where(upd, vals[j], best)
            idx = jnp.where(upd, j, idx)
        return idx

    i1 = argmax4(cand)
    i2 = argmax4([jnp.where(i1 == j, -jnp.inf, cand[j]) for j in range(EXPERTS_PER_GROUP)])
    pick_raw = lambda idx: sum(jnp.where(idx == j, raw[j], 0.0) for j in range(EXPERTS_PER_GROUP))
    g1, g2 = pick_raw(i1), pick_raw(i2)
    tot = g1 + g2
    e1 = gidx * EXPERTS_PER_GROUP + i1
    e2 = gidx * EXPERTS_PER_GROUP + i2
    return e1, e2, g1 / tot, g2 / tot


def _out_ln_route_tail(mix, x_ref, g_ref, b_ref, rw_ref, rb_ref, x1_ref, e_ref, gt_ref):
    z = DEEPNORM_ALPHA * x_ref[...] + mix
    x1 = _layer_norm(z, g_ref[...], b_ref[...])
    x1_ref[...] = x1
    x_hi = x1.astype(BF16)
    x_lo = (x1 - x_hi.astype(F32)).astype(BF16)
    p_hi = jnp.dot(x_hi, rw_ref[...], preferred_element_type=F32)
    p_lo = jnp.dot(x_lo, rw_ref[:, :LANES], preferred_element_type=F32)
    logits = p_hi[:, :LANES] + (p_hi[:, LANES:] + p_lo)
    logits_t = logits.T[0:N_EXPERTS, :]
    e1, e2, g1, g2 = _route(logits_t, rb_ref[...])
    e_ref[0:1, :] = e1
    e_ref[1:2, :] = e2
    gt_ref[0:1, :] = g1
    gt_ref[1:2, :] = g2


def _out_ln_route_a_kernel(a_ref, w_ref, *rest):
    mix = jnp.dot(a_ref[...], w_ref[...], preferred_element_type=F32)
    _out_ln_route_tail(mix, *rest)


def _out_ln_route_b_kernel(o0_ref, o1_ref, o2_ref, l0_ref, l1_ref, l2_ref, w_ref,
                           x_ref, g_ref, b_ref, rw_ref, rb_ref, x1_ref, e_ref, gt_ref,
                           os_ref, ls_ref):
    tm = x_ref.shape[0]
    merged = []
    for hd in range(HEADS_PER_BRANCH):
        for g, (o_ref, l_ref) in enumerate(((o1_ref, l1_ref), (o2_ref, l2_ref))):
            d = BRANCHES[g + 1][1]
            for r in range(d):
                cols = slice(r * BRANCH_WIDTH + hd * HEAD_DIM, r * BRANCH_WIDTH + (hd + 1) * HEAD_DIM)
                os_ref.at[g, hd][pl.ds(r, tm // d, stride=d), :] = o_ref[:, cols].astype(F32)
                ls_ref.at[g, hd][pl.ds(r, tm // d, stride=d), :] = l_ref[:, cols]
        cols = slice(hd * HEAD_DIM, (hd + 1) * HEAD_DIM)
        l0, l1, l2 = l0_ref[:, cols], ls_ref[0, hd], ls_ref[1, hd]
        lm = jnp.maximum(jnp.maximum(l0, l1), l2)
        w0, w1, w2 = jnp.exp(l0 - lm), jnp.exp(l1 - lm), jnp.exp(l2 - lm)
        m = (o0_ref[:, cols].astype(F32) * w0 + os_ref[0, hd] * w1 + os_ref[1, hd] * w2) / (w0 + w1 + w2)
        merged.append(m.astype(BF16))
    mix = jnp.dot(jnp.concatenate(merged, axis=1), w_ref[...], preferred_element_type=F32)
    _out_ln_route_tail(mix, x_ref, g_ref, b_ref, rw_ref, rb_ref, x1_ref, e_ref, gt_ref)


def _out_ln_route(lhs, w_bf, x, ln_g, ln_b, rw_split, rb_col):
    tm = OUT_TM
    kdim = w_bf.shape[0]
    row = lambda width: pl.BlockSpec((tm, width), lambda i: (i, 0))
    full = lambda shape: pl.BlockSpec(shape, lambda i: (0,) * len(shape))
    lane_row = pl.BlockSpec((TOP_K, tm), lambda i: (0, i))
    if len(lhs) == 1:
        body, name, lhs_specs, scratch = _out_ln_route_a_kernel, "out_ln_route_conv", [row(kdim)], []
    else:
        body, name = _out_ln_route_b_kernel, "out_ln_route_attn"
        blocked = [pl.BlockSpec((tm // d, d * BRANCH_WIDTH), lambda i: (i, 0)) for _, d in BRANCHES]
        lhs_specs = blocked + blocked
        scratch = [pltpu.VMEM((N_BRANCH - 1, HEADS_PER_BRANCH, tm, HEAD_DIM), F32)] * 2
    return pl.pallas_call(
        body,
        out_shape=(jax.ShapeDtypeStruct((N_TOK, D_MODEL), F32),
                   jax.ShapeDtypeStruct((TOP_K, N_TOK), jnp.int32),
                   jax.ShapeDtypeStruct((TOP_K, N_TOK), F32)),
        grid=(N_TOK // tm,),
        in_specs=lhs_specs + [
            full((kdim, D_MODEL)), row(D_MODEL), full((1, D_MODEL)), full((1, D_MODEL)),
            full((D_MODEL, 2 * LANES)), full((N_EXPERTS, 1)),
        ],
        out_specs=(row(D_MODEL), lane_row, lane_row),
        scratch_shapes=scratch,
        compiler_params=_params(1),
        name=name,
    )(*lhs, w_bf, x, ln_g.reshape(1, D_MODEL), ln_b.reshape(1, D_MODEL), rw_split, rb_col)


def _build_plan(eidx, gates):
    flat_e = eidx.reshape(-1)
    onehot = (flat_e[:, None] == jnp.arange(N_EXPERTS, dtype=jnp.int32)[None, :]).astype(jnp.int32)
    csum = jnp.cumsum(onehot, axis=0)
    rank = jnp.sum(onehot * csum, axis=1) - 1
    counts = csum[-1]
    nblk_e = ((counts + MOE_BLK - 1) // MOE_BLK).astype(jnp.int32)
    blk_end = jnp.cumsum(nblk_e).astype(jnp.int32)
    blk_start = blk_end - nblk_e
    dest = (blk_start[flat_e] * MOE_BLK + rank).astype(jnp.int32)
    n_valid = blk_end[-1]

    chunks = jnp.full((1,), MOE_CHUNKS, jnp.int32)
    phase_len = jnp.concatenate([chunks, jnp.maximum(nblk_e[:-1], MOE_CHUNKS), nblk_e[-1:]])
    phase_end = jnp.cumsum(phase_len).astype(jnp.int32)
    phase_start = phase_end - phase_len
    step = jnp.arange(MOE_STEPS, dtype=jnp.int32)
    ph = jnp.minimum(jnp.sum((step[:, None] >= phase_end[None, :]).astype(jnp.int32), axis=1), N_EXPERTS)
    off = step - phase_start[ph]
    live = step < phase_end[-1]
    ce = jnp.clip(ph - 1, 0, N_EXPERTS - 1)
    nb_c = jnp.where(ph >= 1, nblk_e[ce], 0)
    comp_on = (live & (ph >= 1) & (off < nb_c)).astype(jnp.int32)
    comp_blk = jnp.clip(jnp.where(ph >= 1, blk_start[ce], 0) + jnp.minimum(off, nb_c - 1), 0, n_valid - 1)
    load_on = (live & (ph < N_EXPERTS) & (off < MOE_CHUNKS)).astype(jnp.int32)
    load_e = jnp.minimum(ph, N_EXPERTS - 1)
    load_c = jnp.where(ph < N_EXPERTS, jnp.minimum(off, MOE_CHUNKS - 1), MOE_CHUNKS - 1)
    sched = (comp_on, comp_blk.astype(jnp.int32), (ph + 1) % 2, load_on, load_e, load_c.astype(jnp.int32), ph % 2)
    return dest[:N_TOK], dest[N_TOK:], blk_end - 1, nblk_e, sched, gates.T


def _dispatch_kernel(d0_ref, d1_ref, last_ref, nblk_ref, x1_ref, xs_hbm, xp_ref, zero_ref, zsem, sem):
    i = pl.program_id(0)
    tm = x1_ref.shape[0]
    blk_rows = MOE_BLK * TOK_PITCH
    _store_token_rows(xp_ref, tm, x1_ref[...])

    @pl.when(i == 0)
    def _():
        zero_ref[...] = jnp.zeros_like(zero_ref)

        def zero_copy(e):
            start = pl.multiple_of(last_ref[e] * blk_rows, blk_rows)
            return pltpu.make_async_copy(zero_ref, xs_hbm.at[pl.ds(start, blk_rows)], zsem)

        for e in range(N_EXPERTS):
            @pl.when(nblk_ref[e] > 0)
            def _():
                zero_copy(e).start()
        for e in range(N_EXPERTS):
            @pl.when(nblk_ref[e] > 0)
            def _():
                zero_copy(e).wait()

    def body(r, carry):
        t = i * tm + r
        src = xp_ref.at[_token_slot(r)]
        for k, d_ref in enumerate((d0_ref, d1_ref)):
            dst = xs_hbm.at[_token_slot(d_ref[t])]
            pltpu.make_async_copy(src, dst, sem).start(priority=k)
        return carry

    lax.fori_loop(0, tm, body, 0, unroll=DMA_UNROLL)
    for _ in range(TOP_K):
        whole = pl.ds(0, tm * TOK_ROWS)
        pltpu.make_async_copy(xp_ref.at[whole], xs_hbm.at[whole], sem).wait()


def _moe_dispatch(xp, dest0, dest1, last_blk, nblk_e):
    tm = DISPATCH_TM
    return pl.pallas_call(
        _dispatch_kernel,
        out_shape=jax.ShapeDtypeStruct((MOE_ROWS * TOK_PITCH, LANES), F32),
        grid_spec=pltpu.PrefetchScalarGridSpec(
            num_scalar_prefetch=4,
            grid=(N_TOK // tm,),
            in_specs=[pl.BlockSpec((tm, D_MODEL), lambda i, *_: (i, 0))],
            out_specs=pl.BlockSpec(memory_space=pl.ANY),
            scratch_shapes=[pltpu.VMEM((tm * TOK_PITCH, LANES), F32),
                            pltpu.VMEM((MOE_BLK * TOK_PITCH, LANES), F32),
                            pltpu.SemaphoreType.DMA, pltpu.SemaphoreType.DMA],
        ),
        compiler_params=_params(1),
        name="moe_dispatch",
    )(dest0, dest1, last_blk, nblk_e, xp)


def _moe_ffn_kernel(con_ref, cblk_ref, cslot_ref, lon_ref, le_ref, lc_ref, lslot_ref,
                    xs_ref, w1_ref, w3_ref, w2_ref, ys_ref, w13_ref, w2b_ref):
    s = pl.program_id(0)
    blk = MOE_BLK
    n_ff = D_FF // LANES

    @pl.when(lon_ref[s] == 1)
    def _():
        slot = lslot_ref[s]
        rows = pl.ds(pl.multiple_of(lc_ref[s] * MOE_UP_ROWS, MOE_UP_ROWS), MOE_UP_ROWS)
        for j in range(n_ff):
            src = slice(j * LANES, (j + 1) * LANES)
            w13_ref[slot, rows, 2 * j * LANES:(2 * j + 1) * LANES] = w1_ref[:, src].astype(BF16)
            w13_ref[slot, rows, (2 * j + 1) * LANES:(2 * j + 2) * LANES] = w3_ref[:, src].astype(BF16)
        rows2 = pl.ds(pl.multiple_of(lc_ref[s] * MOE_DOWN_ROWS, 16), MOE_DOWN_ROWS)
        w2b_ref[slot, rows2, :] = w2_ref[...].astype(BF16)

    @pl.when(con_ref[s] == 1)
    def _():
        slot = cslot_ref[s]
        x = jnp.concatenate([p.astype(BF16) for p in _load_token_rows(xs_ref, blk)], axis=1)
        g = []
        for q in range(0, n_ff, 2):
            npair = min(2, n_ff - q)
            h = jnp.dot(x, w13_ref[slot, :, 2 * q * LANES:2 * (q + npair) * LANES], preferred_element_type=F32)
            for k in range(npair):
                h1, h3 = h[:, 2 * k * LANES:(2 * k + 1) * LANES], h[:, (2 * k + 1) * LANES:(2 * k + 2) * LANES]
                g.append((jax.nn.silu(h1) * h3).astype(BF16))
        y = jnp.dot(jnp.concatenate(g, axis=1), w2b_ref[slot], preferred_element_type=F32)
        _store_token_rows(ys_ref, blk, y)


def _moe_ffn(xs, sched, w1, w3, w2, layer):
    blk = MOE_BLK
    rows = pl.BlockSpec((blk * TOK_PITCH, LANES), lambda s, con, cblk, *_: (cblk[s], 0))
    up = pl.BlockSpec((None, None, MOE_UP_ROWS, D_FF),
                      lambda s, con, cblk, cslot, lon, le, lc, *_: (layer, le[s], lc[s], 0))
    down = pl.BlockSpec((None, None, MOE_DOWN_ROWS, D_MODEL),
                        lambda s, con, cblk, cslot, lon, le, lc, *_: (layer, le[s], lc[s], 0))
    return pl.pallas_call(
        _moe_ffn_kernel,
        out_shape=jax.ShapeDtypeStruct((MOE_ROWS * TOK_PITCH, LANES), F32),
        grid_spec=pltpu.PrefetchScalarGridSpec(
            num_scalar_prefetch=len(sched),
            grid=(MOE_STEPS,),
            in_specs=[rows, up, up, down],
            out_specs=rows,
            scratch_shapes=[pltpu.VMEM((2, D_MODEL, 2 * D_FF), BF16), pltpu.VMEM((2, D_FF, D_MODEL), BF16)],
        ),
        compiler_params=_params(1),
        name="moe_ffn",
    )(*sched, xs, w1, w3, w2)


def _combine_ln_kernel(d0_ref, d1_ref, x1_ref, gt_ref, ys_hbm, g_ref, b_ref, x2_ref, x2b_ref,
                       buf_ref, z_ref, sems):
    i = pl.program_id(0)
    tm = x2_ref.shape[0]

    def issue(tile):
        slot = tile % 2

        def body(r, carry):
            t = tile * tm + r
            for k, d_ref in enumerate((d0_ref, d1_ref)):
                src = ys_hbm.at[_token_slot(d_ref[t])]
                dst = buf_ref.at[slot, k, _token_slot(r)]
                pltpu.make_async_copy(src, dst, sems.at[slot]).start(priority=k)
            return carry

        lax.fori_loop(0, tm, body, 0, unroll=DMA_UNROLL)

    @pl.when(i == 0)
    def _():
        issue(0)

    @pl.when(i + 1 < pl.num_programs(0))
    def _():
        issue(i + 1)

    slot = i % 2
    for k in range(TOP_K):
        whole = pl.ds(0, tm * TOK_ROWS)
        pltpu.make_async_copy(ys_hbm.at[whole], buf_ref.at[slot, k, whole], sems.at[slot]).wait()
    gt = gt_ref[...]
    g0, g1 = gt[:, 0:1], gt[:, 1:2]
    chunks = zip(_load_token_rows(buf_ref.at[slot, 0], tm), _load_token_rows(buf_ref.at[slot, 1], tm))
    for c, (y0, y1) in enumerate(chunks):
        z_ref[:, c * LANES:(c + 1) * LANES] = y0 * g0 + y1 * g1
    x2 = _layer_norm(DEEPNORM_ALPHA * x1_ref[...] + z_ref[...], g_ref[...], b_ref[...])
    x2_ref[...] = x2
    x2b_ref[...] = x2.astype(BF16)


def _combine_ln(x1, ys, dest0, dest1, gates_t, ln_g, ln_b):
    tm = COMB_TM
    row = lambda width: pl.BlockSpec((tm, width), lambda i, d0, d1: (i, 0))
    vec = pl.BlockSpec((1, D_MODEL), lambda i, d0, d1: (0, 0))
    return pl.pallas_call(
        _combine_ln_kernel,
        out_shape=(jax.ShapeDtypeStruct((N_TOK, D_MODEL), F32),
                   jax.ShapeDtypeStruct((N_TOK, D_MODEL), BF16)),
        grid_spec=pltpu.PrefetchScalarGridSpec(
            num_scalar_prefetch=2,
            grid=(N_TOK // tm,),
            in_specs=[row(D_MODEL), row(TOP_K), pl.BlockSpec(memory_space=pl.ANY), vec, vec],
            out_specs=(row(D_MODEL), row(D_MODEL)),
            scratch_shapes=[pltpu.VMEM((2, TOP_K, tm * TOK_PITCH, LANES), F32),
                            pltpu.VMEM((tm, D_MODEL), F32),
                            pltpu.SemaphoreType.DMA((2,))],
        ),
        compiler_params=_params(1),
        name="combine_ln",
    )(dest0, dest1, x1, gates_t, ys, ln_g.reshape(1, D_MODEL), ln_b.reshape(1, D_MODEL))


def kernel(x, a_w_in, a_conv_w, a_w_out, kv_w, b_w_q, b_w_o, router_w, router_bias,
           moe_w1, moe_w3, moe_w2, ln1_g, ln1_b, ln2_g, ln2_b):
    x = x.reshape(N_TOK, D_MODEL)
    xb = x.astype(BF16)
    rw_pad = jnp.pad(router_w.astype(F32), ((0, 0), (0, LANES - N_EXPERTS)))
    rw_hi = rw_pad.astype(BF16)
    rw_lo = (rw_pad - rw_hi.astype(F32)).astype(BF16)
    rw_split = jnp.concatenate([rw_hi, rw_lo], axis=1)
    rb_col = router_bias.astype(F32).reshape(N_EXPERTS, 1)
    k_tables = _rope_tables(1.0)
    q_tables = _rope_tables(1.0 / math.sqrt(HEAD_DIM))
    dils = [d for _, d in BRANCHES]
    k_sh = v_sh = None
    for i in range(DEPTH):
        if i < N_A_LAYERS:
            y = _conv_proj(xb, a_w_in, a_conv_w, i)
            lhs, w_out = [y], a_w_out[i].astype(BF16)
        else:
            j = i - N_A_LAYERS
            if k_sh is None:
                kv3 = kv_w[None]
                k_sh = [_proj(xb, kv3, 0, g, dils[g], k_tables, rope=True) for g in range(N_BRANCH)]
                v_sh = [_proj(xb, kv3, 0, N_BRANCH + g, dils[g], k_tables, rope=False) for g in range(N_BRANCH)]
            q = [_proj(xb, b_w_q, j, g, dils[g], q_tables, rope=True) for g in range(N_BRANCH)]
            outs = [_attn_branch(q[g], k_sh[g], v_sh[g], dils[g]) for g in range(N_BRANCH)]
            lhs, w_out = [o for o, _ in outs] + [l for _, l in outs], b_w_o[j].astype(BF16)
        x1, eidx, gates = _out_ln_route(lhs, w_out, x, ln1_g[i], ln1_b[i], rw_split, rb_col)
        dest0, dest1, last_blk, nblk_e, sched, gates_t = _build_plan(eidx, gates)
        xs = _moe_dispatch(x1, dest0, dest1, last_blk, nblk_e)
        ys = _moe_ffn(xs, sched, moe_w1, moe_w3, moe_w2, i)
        x, xb = _combine_ln(x1, ys, dest0, dest1, gates_t, ln2_g[i], ln2_b[i])
    return x.reshape(BATCH, SEQ, D_MODEL)
```

```python
import functools
import math

import jax
import jax.numpy as jnp
from jax import lax
from jax.experimental import pallas as pl
from jax.experimental.pallas import tpu as pltpu

D_MODEL = 2048
BATCH = 4
SEQ = 4096
DEPTH = 4
N_TOK = BATCH * SEQ
N_A_LAYERS = DEPTH // 2
CONV_WIDTH = 3
BRANCHES = ((128, 1), (512, 4), (2048, 16))
N_BRANCH = len(BRANCHES)
HEADS_PER_BRANCH = 8
HEAD_DIM = 128
BRANCH_WIDTH = HEADS_PER_BRANCH * HEAD_DIM
SPAN = 128
ROT_DIM = HEAD_DIM // 4
ROPE_THETA = 500000.0
N_EXPERTS = 16
N_GROUPS = 4
EXPERTS_PER_GROUP = N_EXPERTS // N_GROUPS
TOP_K = 2
D_FF = 1408
DEEPNORM_ALPHA = (2.0 * DEPTH) ** 0.25
LN_EPS = 1e-5

F32 = jnp.float32
BF16 = jnp.bfloat16

VMEM_LIMIT_BYTES = 56 * 1024 * 1024
LANES = 128
TOK_ROWS = D_MODEL // LANES
TOK_PITCH = TOK_ROWS + 4
TOK_ALIGN = math.gcd(TOK_PITCH, 8)
BF16_TILE_ROWS = 16
SPLIT_STRIDE = 4

CONV_TM, CONV_TN = 2048, 256
PROJ_TM = 1024
OUT_TM = 256
ATTN_QB = 1024
MOE_BLK = 256
MOE_NBLK = (N_TOK * TOP_K) // MOE_BLK + N_EXPERTS
MOE_ROWS = MOE_NBLK * MOE_BLK
MOE_CHUNKS = 8
MOE_UP_ROWS = D_MODEL // MOE_CHUNKS
MOE_DOWN_ROWS = D_FF // MOE_CHUNKS
MOE_STEPS = MOE_NBLK + N_EXPERTS * MOE_CHUNKS
DISPATCH_TM = 512
COMB_TM = 512
DMA_UNROLL = 8


def _params(n_axes):
    return pltpu.CompilerParams(
        dimension_semantics=("arbitrary",) * n_axes,
        vmem_limit_bytes=VMEM_LIMIT_BYTES,
    )


def _store_token_rows(dst_ref, rows, v):
    for c in range(TOK_ROWS):
        dst_ref[pl.ds(c, rows, stride=TOK_PITCH), :] = v[:, c * LANES:(c + 1) * LANES]


def _load_token_rows(src_ref, rows):
    return [src_ref[pl.ds(c, rows, stride=TOK_PITCH), :] for c in range(TOK_ROWS)]


def _token_slot(t):
    return pl.ds(pl.multiple_of(t * TOK_PITCH, TOK_ALIGN), TOK_ROWS)


def _conv_proj_kernel(x_ref, wb_ref, wc_ref, wh_ref, cw_ref, y_ref, wbf_ref, ext_ref):
    i = pl.program_id(1)
    tm = x_ref.shape[0]

    @pl.when(i == 0)
    def _():
        wbf_ref[0] = wb_ref[...].astype(BF16)
        wbf_ref[1] = wc_ref[...].astype(BF16)
        wbf_ref[2] = wh_ref[...].astype(BF16)

    @pl.when(i % (SEQ // tm) == 0)
    def _():
        ext_ref[0:8, :] = jnp.zeros((8, ext_ref.shape[1]), F32)

    cw = cw_ref[...]
    x = x_ref[...]
    gate_b = jnp.dot(x, wbf_ref[0], preferred_element_type=F32)
    gate_c = jnp.dot(x, wbf_ref[1], preferred_element_type=F32)
    h = jnp.dot(x, wbf_ref[2], preferred_element_type=F32)
    u = gate_c * h
    ext_ref[8:8 + tm, :] = u
    u1 = ext_ref[7:7 + tm, :]
    u2 = ext_ref[6:6 + tm, :]
    conv = cw[2:3, :] * u + cw[1:2, :] * u1 + cw[0:1, :] * u2
    y_ref[...] = (gate_b * conv).astype(BF16)
    ext_ref[0:8, :] = ext_ref[tm:tm + 8, :]


def _conv_proj(xb, w_in, conv_w, layer):
    tm, tn = CONV_TM, CONV_TN
    nj = D_MODEL // tn
    wspec = lambda off: pl.BlockSpec((None, D_MODEL, tn), lambda j, i: (layer, 0, j + off * nj))
    return pl.pallas_call(
        _conv_proj_kernel,
        out_shape=jax.ShapeDtypeStruct((N_TOK, D_MODEL), BF16),
        grid=(nj, N_TOK // tm),
        in_specs=[
            pl.BlockSpec((tm, D_MODEL), lambda j, i: (i, 0)),
            wspec(0), wspec(1), wspec(2),
            pl.BlockSpec((None, CONV_WIDTH, tn), lambda j, i: (layer, 0, j)),
        ],
        out_specs=pl.BlockSpec((tm, tn), lambda j, i: (i, j)),
        scratch_shapes=[pltpu.VMEM((3, D_MODEL, tn), BF16), pltpu.VMEM((tm + 8, tn), F32)],
        compiler_params=_params(2),
        name="conv_proj",
    )(xb, w_in, w_in, w_in, conv_w)


def _proj_kernel(x_ref, w_ref, c_ref, s1_ref, s2_ref, o_ref, wbf_ref, rot_ref, *maybe_rot2_ref, rope, dilation):
    tm = x_ref.shape[0]
    rot2_ref = maybe_rot2_ref[0] if maybe_rot2_ref else None

    @pl.when(pl.program_id(0) == 0)
    def _():
        wbf_ref[...] = w_ref[...].astype(BF16)

    acc = jnp.dot(x_ref[...], wbf_ref[...], preferred_element_type=F32)
    if rope:
        c = c_ref[...]
        s1 = s1_ref[...]
        s2 = s2_ref[...]
    for hd in range(HEADS_PER_BRANCH):
        sl = slice(hd * HEAD_DIM, (hd + 1) * HEAD_DIM)
        t = acc[:, sl]
        if rope:
            t = t * c + pltpu.roll(t, HEAD_DIM - ROT_DIM // 2, 1) * s1 + pltpu.roll(t, ROT_DIM // 2, 1) * s2
        if dilation == 1:
            o_ref[:, sl] = t.astype(BF16)
            continue
        rot_ref[hd] = t
        for r in range(dilation):
            if dilation <= SPLIT_STRIDE:
                piece = rot_ref.at[hd][pl.ds(r, tm // dilation, stride=dilation), :]
            else:
                q, j = divmod(r, SPLIT_STRIDE)
                if q == 0:
                    rot2_ref[hd, j] = rot_ref.at[hd][pl.ds(j, tm // SPLIT_STRIDE, stride=SPLIT_STRIDE), :]
                piece = rot2_ref.at[hd, j][pl.ds(q, tm // dilation, stride=dilation // SPLIT_STRIDE), :]
            o_ref[:, r * BRANCH_WIDTH + hd * HEAD_DIM:r * BRANCH_WIDTH + (hd + 1) * HEAD_DIM] = piece.astype(BF16)


def _proj(xb, w3d, layer, col, dilation, tables, rope):
    tm = PROJ_TM
    w = BRANCH_WIDTH
    tab = pl.BlockSpec((tm, HEAD_DIM), lambda i: (i % (SEQ // tm), 0))
    scratch = [pltpu.VMEM((D_MODEL, w), BF16), pltpu.VMEM((HEADS_PER_BRANCH, tm, HEAD_DIM), F32)]
    if dilation > SPLIT_STRIDE:
        scratch.append(pltpu.VMEM((HEADS_PER_BRANCH, SPLIT_STRIDE, tm // SPLIT_STRIDE, HEAD_DIM), F32))
    return pl.pallas_call(
        functools.partial(_proj_kernel, rope=rope, dilation=dilation),
        out_shape=jax.ShapeDtypeStruct((N_TOK // dilation, dilation * w), BF16),
        grid=(N_TOK // tm,),
        in_specs=[
            pl.BlockSpec((tm, D_MODEL), lambda i: (i, 0)),
            pl.BlockSpec((None, D_MODEL, w), lambda i: (layer, 0, col)),
            tab, tab, tab,
        ],
        out_specs=pl.BlockSpec((tm // dilation, dilation * w), lambda i: (i, 0)),
        scratch_shapes=scratch,
        compiler_params=_params(1),
        name=("proj_rope" if rope else "proj") + f"_d{dilation}",
    )(xb, w3d, *tables)


def _rope_tables(scale):
    inv_freq = ROPE_THETA ** (-jnp.arange(0, ROT_DIM, 2, dtype=F32) / ROT_DIM)
    ang = jnp.arange(SEQ, dtype=F32)[:, None] * inv_freq[None, :]
    cos, sin = jnp.cos(ang), jnp.sin(ang)
    half = ROT_DIM // 2
    rest = HEAD_DIM - ROT_DIM
    c = jnp.concatenate([cos, cos, jnp.ones((SEQ, rest), F32)], axis=1) * scale
    s1 = jnp.concatenate([-sin, jnp.zeros((SEQ, HEAD_DIM - half), F32)], axis=1) * scale
    s2 = jnp.concatenate([jnp.zeros((SEQ, half), F32), sin, jnp.zeros((SEQ, rest), F32)], axis=1) * scale
    return c, s1, s2


def _attn_kernel(q_ref, kc_ref, kp_ref, vc_ref, vp_ref, o_ref, l_ref):
    qb = q_ref.shape[0]
    qi = lax.broadcasted_iota(jnp.int32, (SPAN, 2 * SPAN), 0)
    kj = lax.broadcasted_iota(jnp.int32, (SPAN, 2 * SPAN), 1)
    dist = SPAN + qi - kj
    band = (dist >= 0) & (dist <= SPAN)
    first_key = jnp.where(pl.program_id(1) == 0, SPAN, 0)
    band_first = band & (kj >= first_key)
    for a in range(qb // SPAN):
        rows = slice(a * SPAN, (a + 1) * SPAN)
        for hd in range(q_ref.shape[1] // HEAD_DIM):
            cols = slice(hd * HEAD_DIM, (hd + 1) * HEAD_DIM)
            q = q_ref[rows, cols]
            if a == 0:
                k_prev, v_prev = kp_ref[:, cols], vp_ref[:, cols]
                mask = band_first
            else:
                prows = slice((a - 1) * SPAN, a * SPAN)
                k_prev, v_prev = kc_ref[prows, cols], vc_ref[prows, cols]
                mask = band
            kk = jnp.concatenate([k_prev, kc_ref[rows, cols]], axis=0)
            vv = jnp.concatenate([v_prev, vc_ref[rows, cols]], axis=0)
            s = lax.dot_general(q, kk, (((1,), (1,)), ((), ())), preferred_element_type=F32)
            s = jnp.where(mask, s, -jnp.inf)
            m = jnp.max(s, axis=-1, keepdims=True)
            p = jnp.exp(s - m)
            den = jnp.sum(p, axis=-1, keepdims=True)
            o = jnp.dot(p.astype(BF16), vv, preferred_element_type=F32) / den
            o_ref[rows, cols] = o.astype(BF16)
            l_ref[rows, cols] = jnp.broadcast_to(m + jnp.log(den), (SPAN, HEAD_DIM))


def _attn_branch(q, k, v, dilation):
    sub = SEQ // dilation
    qb = min(ATTN_QB, sub)
    n_res = min(dilation, ATTN_QB // qb)
    w = n_res * BRANCH_WIDTH
    view = lambda t: t.reshape(BATCH, sub, dilation * BRANCH_WIDTH)
    cur = pl.BlockSpec((None, qb, w), lambda b, n, r: (b, n, r))
    prev = pl.BlockSpec((None, SPAN, w),
                        lambda b, n, r: (b, jnp.maximum(n * (qb // SPAN) - 1, 0), r))
    o, l = pl.pallas_call(
        _attn_kernel,
        out_shape=(jax.ShapeDtypeStruct((BATCH, sub, dilation * BRANCH_WIDTH), BF16),
                   jax.ShapeDtypeStruct((BATCH, sub, dilation * BRANCH_WIDTH), F32)),
        grid=(BATCH, sub // qb, dilation // n_res),
        in_specs=[cur, cur, prev, cur, prev],
        out_specs=(cur, cur),
        compiler_params=_params(3),
        name=f"dilated_attn_d{dilation}",
    )(view(q), view(k), view(k), view(v), view(v))
    width = dilation * BRANCH_WIDTH
    return o.reshape(N_TOK // dilation, width), l.reshape(N_TOK // dilation, width)


def _layer_norm(z, g, b):
    mean = jnp.mean(z, axis=-1, keepdims=True)
    zc = z - mean
    var = jnp.mean(zc * zc, axis=-1, keepdims=True)
    return zc * lax.rsqrt(var + LN_EPS) * g + b


def _route(logits_t, bias_col):
    scores = jax.nn.sigmoid(logits_t)
    sel = scores + bias_col
    sel_r = [sel[e:e + 1, :] for e in range(N_EXPERTS)]
    sc_r = [scores[e:e + 1, :] for e in range(N_EXPERTS)]
    best_g = None
    for g in range(N_GROUPS):
        v = sel_r[g * EXPERTS_PER_GROUP:(g + 1) * EXPERTS_PER_GROUP]
        gs = None
        for a in range(EXPERTS_PER_GROUP):
            for b in range(a + 1, EXPERTS_PER_GROUP):
                ps = v[a] + v[b]
                gs = ps if gs is None else jnp.maximum(gs, ps)
        if best_g is None:
            best_g, gidx = gs, jnp.zeros_like(gs, dtype=jnp.int32)
        else:
            upd = gs > best_g
            best_g = jnp.where(upd, gs, best_g)
            gidx = jnp.where(upd, g, gidx)
    cand, raw = [], []
    for j in range(EXPERTS_PER_GROUP):
        cv, rv = sel_r[j], sc_r[j]
        for g in range(1, N_GROUPS):
            pick = gidx == g
            cv = jnp.where(pick, sel_r[g * EXPERTS_PER_GROUP + j], cv)
            rv = jnp.where(pick, sc_r[g * EXPERTS_PER_GROUP + j], rv)
        cand.append(cv)
        raw.append(rv)

    def argmax4(vals):
        best, idx = vals[0], jnp.zeros_like(gidx)
        for j in range(1, EXPERTS_PER_GROUP):
            upd = vals[j] > best
            best = jnp.where(upd, vals[j], best)
            idx = jnp.where(upd, j, idx)
        return idx

    i1 = argmax4(cand)
    i2 = argmax4([jnp.where(i1 == j, -jnp.inf, cand[j]) for j in range(EXPERTS_PER_GROUP)])
    pick_raw = lambda idx: sum(jnp.where(idx == j, raw[j], 0.0) for j in range(EXPERTS_PER_GROUP))
    g1, g2 = pick_raw(i1), pick_raw(i2)
    tot = g1 + g2
    e1 = gidx * EXPERTS_PER_GROUP + i1
    e2 = gidx * EXPERTS_PER_GROUP + i2
    return e1, e2, g1 / tot, g2 / tot


def _out_ln_route_tail(mix, x_ref, g_ref, b_ref, rw_ref, rb_ref, x1_ref, e_ref, gt_ref):
    z = DEEPNORM_ALPHA * x_ref[...] + mix
    x1 = _layer_norm(z, g_ref[...], b_ref[...])
    x1_ref[...] = x1
    x_hi = x1.astype(BF16)
    x_lo = (x1 - x_hi.astype(F32)).astype(BF16)
    p_hi = jnp.dot(x_hi, rw_ref[...], preferred_element_type=F32)
    p_lo = jnp.dot(x_lo, rw_ref[:, :LANES], preferred_element_type=F32)
    logits = p_hi[:, :LANES] + (p_hi[:, LANES:] + p_lo)
    logits_t = logits.T[0:N_EXPERTS, :]
    e1, e2, g1, g2 = _route(logits_t, rb_ref[...])
    e_ref[0:1, :] = e1
    e_ref[1:2, :] = e2
    gt_ref[0:1, :] = g1
    gt_ref[1:2, :] = g2


def _out_ln_route_a_kernel(a_ref, w_ref, *rest):
    mix = jnp.dot(a_ref[...], w_ref[...], preferred_element_type=F32)
    _out_ln_route_tail(mix, *rest)


def _out_ln_route_b_kernel(o0_ref, o1_ref, o2_ref, l0_ref, l1_ref, l2_ref, w_ref,
                           x_ref, g_ref, b_ref, rw_ref, rb_ref, x1_ref, e_ref, gt_ref,
                           os_ref, ls_ref):
    tm = x_ref.shape[0]
    merged = []
    for hd in range(HEADS_PER_BRANCH):
        for g, (o_ref, l_ref) in enumerate(((o1_ref, l1_ref), (o2_ref, l2_ref))):
            d = BRANCHES[g + 1][1]
            for r in range(d):
                cols = slice(r * BRANCH_WIDTH + hd * HEAD_DIM, r * BRANCH_WIDTH + (hd + 1) * HEAD_DIM)
                os_ref.at[g, hd][pl.ds(r, tm // d, stride=d), :] = o_ref[:, cols].astype(F32)
                ls_ref.at[g, hd][pl.ds(r, tm // d, stride=d), :] = l_ref[:, cols]
        cols = slice(hd * HEAD_DIM, (hd + 1) * HEAD_DIM)
        l0, l1, l2 = l0_ref[:, cols], ls_ref[0, hd], ls_ref[1, hd]
        lm = jnp.maximum(jnp.maximum(l0, l1), l2)
        w0, w1, w2 = jnp.exp(l0 - lm), jnp.exp(l1 - lm), jnp.exp(l2 - lm)
        m = (o0_ref[:, cols].astype(F32) * w0 + os_ref[0, hd] * w1 + os_ref[1, hd] * w2) / (w0 + w1 + w2)
        merged.append(m.astype(BF16))
    mix = jnp.dot(jnp.concatenate(merged, axis=1), w_ref[...], preferred_element_type=F32)
    _out_ln_route_tail(mix, x_ref, g_ref, b_ref, rw_ref, rb_ref, x1_ref, e_ref, gt_ref)


def _out_ln_route(lhs, w_bf, x, ln_g, ln_b, rw_split, rb_col):
    tm = OUT_TM
    kdim = w_bf.shape[0]
    row = lambda width: pl.BlockSpec((tm, width), lambda i: (i, 0))
    full = lambda shape: pl.BlockSpec(shape, lambda i: (0,) * len(shape))
    lane_row = pl.BlockSpec((TOP_K, tm), lambda i: (0, i))
    if len(lhs) == 1:
        body, name, lhs_specs, scratch = _out_ln_route_a_kernel, "out_ln_route_conv", [row(kdim)], []
    else:
        body, name = _out_ln_route_b_kernel, "out_ln_route_attn"
        blocked = [pl.BlockSpec((tm // d, d * BRANCH_WIDTH), lambda i: (i, 0)) for _, d in BRANCHES]
        lhs_specs = blocked + blocked
        scratch = [pltpu.VMEM((N_BRANCH - 1, HEADS_PER_BRANCH, tm, HEAD_DIM), F32)] * 2
    return pl.pallas_call(
        body,
        out_shape=(jax.ShapeDtypeStruct((N_TOK, D_MODEL), F32),
                   jax.ShapeDtypeStruct((TOP_K, N_TOK), jnp.int32),
                   jax.ShapeDtypeStruct((TOP_K, N_TOK), F32)),
        grid=(N_TOK // tm,),
        in_specs=lhs_specs + [
            full((kdim, D_MODEL)), row(D_MODEL), full((1, D_MODEL)), full((1, D_MODEL)),
            full((D_MODEL, 2 * LANES)), full((N_EXPERTS, 1)),
        ],
        out_specs=(row(D_MODEL), lane_row, lane_row),
        scratch_shapes=scratch,
        compiler_params=_params(1),
        name=name,
    )(*lhs, w_bf, x, ln_g.reshape(1, D_MODEL), ln_b.reshape(1, D_MODEL), rw_split, rb_col)


def _build_plan(eidx, gates):
    flat_e = eidx.reshape(-1)
    onehot = (flat_e[:, None] == jnp.arange(N_EXPERTS, dtype=jnp.int32)[None, :]).astype(jnp.int32)
    csum = jnp.cumsum(onehot, axis=0)
    rank = jnp.sum(onehot * csum, axis=1) - 1
    counts = csum[-1]
    nblk_e = ((counts + MOE_BLK - 1) // MOE_BLK).astype(jnp.int32)
    blk_end = jnp.cumsum(nblk_e).astype(jnp.int32)
    blk_start = blk_end - nblk_e
    dest = (blk_start[flat_e] * MOE_BLK + rank).astype(jnp.int32)
    n_valid = blk_end[-1]

    chunks = jnp.full((1,), MOE_CHUNKS, jnp.int32)
    phase_len = jnp.concatenate([chunks, jnp.maximum(nblk_e[:-1], MOE_CHUNKS), nblk_e[-1:]])
    phase_end = jnp.cumsum(phase_len).astype(jnp.int32)
    phase_start = phase_end - phase_len
    step = jnp.arange(MOE_STEPS, dtype=jnp.int32)
    ph = jnp.minimum(jnp.sum((step[:, None] >= phase_end[None, :]).astype(jnp.int32), axis=1), N_EXPERTS)
    off = step - phase_start[ph]
    live = step < phase_end[-1]
    ce = jnp.clip(ph - 1, 0, N_EXPERTS - 1)
    nb_c = jnp.where(ph >= 1, nblk_e[ce], 0)
    comp_on = (live & (ph >= 1) & (off < nb_c)).astype(jnp.int32)
    comp_blk = jnp.clip(jnp.where(ph >= 1, blk_start[ce], 0) + jnp.minimum(off, nb_c - 1), 0, n_valid - 1)
    load_on = (live & (ph < N_EXPERTS) & (off < MOE_CHUNKS)).astype(jnp.int32)
    load_e = jnp.minimum(ph, N_EXPERTS - 1)
    load_c = jnp.where(ph < N_EXPERTS, jnp.minimum(off, MOE_CHUNKS - 1), MOE_CHUNKS - 1)
    sched = (comp_on, comp_blk.astype(jnp.int32), (ph + 1) % 2, load_on, load_e, load_c.astype(jnp.int32), ph % 2)
    return dest[:N_TOK], dest[N_TOK:], blk_end - 1, nblk_e, sched, gates.T


def _dispatch_kernel(d0_ref, d1_ref, last_ref, nblk_ref, x1_ref, xs_hbm, xp_ref, zero_ref, zsem, sem):
    i = pl.program_id(0)
    tm = x1_ref.shape[0]
    blk_rows = MOE_BLK * TOK_PITCH
    _store_token_rows(xp_ref, tm, x1_ref[...])

    @pl.when(i == 0)
    def _():
        zero_ref[...] = jnp.zeros_like(zero_ref)

        def zero_copy(e):
            start = pl.multiple_of(last_ref[e] * blk_rows, blk_rows)
            return pltpu.make_async_copy(zero_ref, xs_hbm.at[pl.ds(start, blk_rows)], zsem)

        for e in range(N_EXPERTS):
            @pl.when(nblk_ref[e] > 0)
            def _():
                zero_copy(e).start()
        for e in range(N_EXPERTS):
            @pl.when(nblk_ref[e] > 0)
            def _():
                zero_copy(e).wait()

    def body(r, carry):
        t = i * tm + r
        src = xp_ref.at[_token_slot(r)]
        for k, d_ref in enumerate((d0_ref, d1_ref)):
            dst = xs_hbm.at[_token_slot(d_ref[t])]
            pltpu.make_async_copy(src, dst, sem).start(priority=k)
        return carry

    lax.fori_loop(0, tm, body, 0, unroll=DMA_UNROLL)
    for _ in range(TOP_K):
        whole = pl.ds(0, tm * TOK_ROWS)
        pltpu.make_async_copy(xp_ref.at[whole], xs_hbm.at[whole], sem).wait()


def _moe_dispatch(xp, dest0, dest1, last_blk, nblk_e):
    tm = DISPATCH_TM
    return pl.pallas_call(
        _dispatch_kernel,
        out_shape=jax.ShapeDtypeStruct((MOE_ROWS * TOK_PITCH, LANES), F32),
        grid_spec=pltpu.PrefetchScalarGridSpec(
            num_scalar_prefetch=4,
            grid=(N_TOK // tm,),
            in_specs=[pl.BlockSpec((tm, D_MODEL), lambda i, *_: (i, 0))],
            out_specs=pl.BlockSpec(memory_space=pl.ANY),
            scratch_shapes=[pltpu.VMEM((tm * TOK_PITCH, LANES), F32),
                            pltpu.VMEM((MOE_BLK * TOK_PITCH, LANES), F32),
                            pltpu.SemaphoreType.DMA, pltpu.SemaphoreType.DMA],
        ),
        compiler_params=_params(1),
        name="moe_dispatch",
    )(dest0, dest1, last_blk, nblk_e, xp)


def _moe_ffn_kernel(con_ref, cblk_ref, cslot_ref, lon_ref, le_ref, lc_ref, lslot_ref,
                    xs_ref, w1_ref, w3_ref, w2_ref, ys_ref, w13a_ref, w13b_ref, w2a_ref, w2b_ref):
    s = pl.program_id(0)
    blk = MOE_BLK
    n_ff = D_FF // LANES
    slots = ((w13a_ref, w2a_ref), (w13b_ref, w2b_ref))

    def cast_chunk(w13_ref, w2s_ref):
        rows = pl.ds(pl.multiple_of(lc_ref[s] * MOE_UP_ROWS, MOE_UP_ROWS), MOE_UP_ROWS)
        for j in range(n_ff):
            src = slice(j * LANES, (j + 1) * LANES)
            w13_ref[rows, 2 * j * LANES:(2 * j + 1) * LANES] = w1_ref[:, src].astype(BF16)
            w13_ref[rows, (2 * j + 1) * LANES:(2 * j + 2) * LANES] = w3_ref[:, src].astype(BF16)
        rows2 = pl.ds(pl.multiple_of(lc_ref[s] * MOE_DOWN_ROWS, BF16_TILE_ROWS), MOE_DOWN_ROWS)
        w2s_ref[rows2, :] = w2_ref[...].astype(BF16)

    def compute(w13_ref, w2s_ref):
        x = jnp.concatenate([p.astype(BF16) for p in _load_token_rows(xs_ref, blk)], axis=1)
        g = []
        for q in range(0, n_ff, 2):
            npair = min(2, n_ff - q)
            h = jnp.dot(x, w13_ref[:, 2 * q * LANES:2 * (q + npair) * LANES], preferred_element_type=F32)
            for k in range(npair):
                h1, h3 = h[:, 2 * k * LANES:(2 * k + 1) * LANES], h[:, (2 * k + 1) * LANES:(2 * k + 2) * LANES]
                g.append((jax.nn.silu(h1) * h3).astype(BF16))
        y = jnp.dot(jnp.concatenate(g, axis=1), w2s_ref[...], preferred_element_type=F32)
        _store_token_rows(ys_ref, blk, y)

    load_on, comp_on = lon_ref[s] == 1, con_ref[s] == 1
    for p in (0, 1):
        here = lslot_ref[s] == p

        @pl.when(here & load_on & comp_on)
        def _():
            compute(*slots[1 - p])
            cast_chunk(*slots[p])

        @pl.when(here & load_on & jnp.logical_not(comp_on))
        def _():
            cast_chunk(*slots[p])

        @pl.when(here & jnp.logical_not(load_on) & comp_on)
        def _():
            compute(*slots[1 - p])


def _moe_ffn(xs, sched, w1, w3, w2, layer):
    blk = MOE_BLK
    rows = pl.BlockSpec((blk * TOK_PITCH, LANES), lambda s, con, cblk, *_: (cblk[s], 0))
    up = pl.BlockSpec((None, None, MOE_UP_ROWS, D_FF),
                      lambda s, con, cblk, cslot, lon, le, lc, *_: (layer, le[s], lc[s], 0))
    down = pl.BlockSpec((None, None, MOE_DOWN_ROWS, D_MODEL),
                        lambda s, con, cblk, cslot, lon, le, lc, *_: (layer, le[s], lc[s], 0))
    return pl.pallas_call(
        _moe_ffn_kernel,
        out_shape=jax.ShapeDtypeStruct((MOE_ROWS * TOK_PITCH, LANES), F32),
        grid_spec=pltpu.PrefetchScalarGridSpec(
            num_scalar_prefetch=len(sched),
            grid=(MOE_STEPS,),
            in_specs=[rows, up, up, down],
            out_specs=rows,
            scratch_shapes=[pltpu.VMEM((D_MODEL, 2 * D_FF), BF16), pltpu.VMEM((D_MODEL, 2 * D_FF), BF16),
                            pltpu.VMEM((D_FF, D_MODEL), BF16), pltpu.VMEM((D_FF, D_MODEL), BF16)],
        ),
        compiler_params=_params(1),
        name="moe_ffn",
    )(*sched, xs, w1, w3, w2)


def _combine_ln_kernel(d0_ref, d1_ref, x1_ref, gt_ref, ys_hbm, g_ref, b_ref, x2_ref, x2b_ref,
                       buf_ref, z_ref, sems):
    i = pl.program_id(0)
    tm = x2_ref.shape[0]

    def issue(tile):
        slot = tile % 2

        def body(r, carry):
            t = tile * tm + r
            for k, d_ref in enumerate((d0_ref, d1_ref)):
                src = ys_hbm.at[_token_slot(d_ref[t])]
                dst = buf_ref.at[slot, k, _token_slot(r)]
                pltpu.make_async_copy(src, dst, sems.at[slot]).start(priority=k)
            return carry

        lax.fori_loop(0, tm, body, 0, unroll=DMA_UNROLL)

    @pl.when(i == 0)
    def _():
        issue(0)

    @pl.when(i + 1 < pl.num_programs(0))
    def _():
        issue(i + 1)

    slot = i % 2
    for k in range(TOP_K):
        whole = pl.ds(0, tm * TOK_ROWS)
        pltpu.make_async_copy(ys_hbm.at[whole], buf_ref.at[slot, k, whole], sems.at[slot]).wait()
    gt = gt_ref[...]
    g0, g1 = gt[:, 0:1], gt[:, 1:2]
    chunks = zip(_load_token_rows(buf_ref.at[slot, 0], tm), _load_token_rows(buf_ref.at[slot, 1], tm))
    for c, (y0, y1) in enumerate(chunks):
        z_ref[:, c * LANES:(c + 1) * LANES] = y0 * g0 + y1 * g1
    x2 = _layer_norm(DEEPNORM_ALPHA * x1_ref[...] + z_ref[...], g_ref[...], b_ref[...])
    x2_ref[...] = x2
    x2b_ref[...] = x2.astype(BF16)


def _combine_ln(x1, ys, dest0, dest1, gates_t, ln_g, ln_b):
    tm = COMB_TM
    row = lambda width: pl.BlockSpec((tm, width), lambda i, d0, d1: (i, 0))
    vec = pl.BlockSpec((1, D_MODEL), lambda i, d0, d1: (0, 0))
    return pl.pallas_call(
        _combine_ln_kernel,
        out_shape=(jax.ShapeDtypeStruct((N_TOK, D_MODEL), F32),
                   jax.ShapeDtypeStruct((N_TOK, D_MODEL), BF16)),
        grid_spec=pltpu.PrefetchScalarGridSpec(
            num_scalar_prefetch=2,
            grid=(N_TOK // tm,),
            in_specs=[row(D_MODEL), row(TOP_K), pl.BlockSpec(memory_space=pl.ANY), vec, vec],
            out_specs=(row(D_MODEL), row(D_MODEL)),
            scratch_shapes=[pltpu.VMEM((2, TOP_K, tm * TOK_PITCH, LANES), F32),
                            pltpu.VMEM((tm, D_MODEL), F32),
                            pltpu.SemaphoreType.DMA((2,))],
        ),
        compiler_params=_params(1),
        name="combine_ln",
    )(dest0, dest1, x1, gates_t, ys, ln_g.reshape(1, D_MODEL), ln_b.reshape(1, D_MODEL))


def kernel(x, a_w_in, a_conv_w, a_w_out, kv_w, b_w_q, b_w_o, router_w, router_bias,
           moe_w1, moe_w3, moe_w2, ln1_g, ln1_b, ln2_g, ln2_b):
    x = x.reshape(N_TOK, D_MODEL)
    xb = x.astype(BF16)
    rw_pad = jnp.pad(router_w.astype(F32), ((0, 0), (0, LANES - N_EXPERTS)))
    rw_hi = rw_pad.astype(BF16)
    rw_lo = (rw_pad - rw_hi.astype(F32)).astype(BF16)
    rw_split = jnp.concatenate([rw_hi, rw_lo], axis=1)
    rb_col = router_bias.astype(F32).reshape(N_EXPERTS, 1)
    k_tables = _rope_tables(1.0)
    q_tables = _rope_tables(1.0 / math.sqrt(HEAD_DIM))
    dils = [d for _, d in BRANCHES]
    k_sh = v_sh = None
    for i in range(DEPTH):
        if i < N_A_LAYERS:
            y = _conv_proj(xb, a_w_in, a_conv_w, i)
            lhs, w_out = [y], a_w_out[i].astype(BF16)
        else:
            j = i - N_A_LAYERS
            if k_sh is None:
                kv3 = kv_w[None]
                k_sh = [_proj(xb, kv3, 0, g, dils[g], k_tables, rope=True) for g in range(N_BRANCH)]
                v_sh = [_proj(xb, kv3, 0, N_BRANCH + g, dils[g], k_tables, rope=False) for g in range(N_BRANCH)]
            q = [_proj(xb, b_w_q, j, g, dils[g], q_tables, rope=True) for g in range(N_BRANCH)]
            outs = [_attn_branch(q[g], k_sh[g], v_sh[g], dils[g]) for g in range(N_BRANCH)]
            lhs, w_out = [o for o, _ in outs] + [l for _, l in outs], b_w_o[j].astype(BF16)
        x1, eidx, gates = _out_ln_route(lhs, w_out, x, ln1_g[i], ln1_b[i], rw_split, rb_col)
        dest0, dest1, last_blk, nblk_e, sched, gates_t = _build_plan(eidx, gates)
        xs = _moe_dispatch(x1, dest0, dest1, last_blk, nblk_e)
        ys = _moe_ffn(xs, sched, moe_w1, moe_w3, moe_w2, i)
        x, xb = _combine_ln(x1, ys, dest0, dest1, gates_t, ln2_g[i], ln2_b[i])
    return x.reshape(BATCH, SEQ, D_MODEL)
```

```python
import functools
import math

import jax
import jax.numpy as jnp
from jax import lax
from jax.experimental import pallas as pl
from jax.experimental.pallas import tpu as pltpu

D_MODEL = 2048
BATCH = 4
SEQ = 4096
DEPTH = 4
N_TOK = BATCH * SEQ
N_A_LAYERS = DEPTH // 2
CONV_WIDTH = 3
BRANCHES = ((128, 1), (512, 4), (2048, 16))
N_BRANCH = len(BRANCHES)
HEADS_PER_BRANCH = 8
HEAD_DIM = 128
BRANCH_WIDTH = HEADS_PER_BRANCH * HEAD_DIM
SPAN = 128
ROT_DIM = HEAD_DIM // 4
ROPE_THETA = 500000.0
N_EXPERTS = 16
N_GROUPS = 4
EXPERTS_PER_GROUP = N_EXPERTS // N_GROUPS
TOP_K = 2
D_FF = 1408
DEEPNORM_ALPHA = (2.0 * DEPTH) ** 0.25
LN_EPS = 1e-5

F32 = jnp.float32
BF16 = jnp.bfloat16

VMEM_LIMIT_BYTES = 56 * 1024 * 1024
LANES = 128
TOK_ROWS = D_MODEL // LANES
TOK_PITCH = TOK_ROWS + 4
SPLIT_STRIDE = 4

CONV_TM, CONV_TN = 2048, 256
PROJ_TM = 1024
OUT_TM = 256
ATTN_QB = 1024
MOE_BLK = 256
MOE_NBLK = (N_TOK * TOP_K) // MOE_BLK + N_EXPERTS
MOE_ROWS = MOE_NBLK * MOE_BLK
MOE_CHUNKS = 8
MOE_UP_ROWS = D_MODEL // MOE_CHUNKS
MOE_DOWN_ROWS = D_FF // MOE_CHUNKS
MOE_STEPS = MOE_NBLK + N_EXPERTS * MOE_CHUNKS
DISPATCH_TM = 512
COMB_TM = 512
DMA_UNROLL = 8


def _params(n_axes):
    return pltpu.CompilerParams(
        dimension_semantics=("arbitrary",) * n_axes,
        vmem_limit_bytes=VMEM_LIMIT_BYTES,
    )


def _store_token_rows(dst_ref, rows, v):
    for c in range(TOK_ROWS):
        dst_ref[pl.ds(c, rows, stride=TOK_PITCH), :] = v[:, c * LANES:(c + 1) * LANES]


def _load_token_rows(src_ref, rows):
    return [src_ref[pl.ds(c, rows, stride=TOK_PITCH), :] for c in range(TOK_ROWS)]


def _token_slot(t):
    return pl.ds(pl.multiple_of(t * TOK_PITCH, 4), TOK_ROWS)


def _conv_proj_kernel(x_ref, wb_ref, wc_ref, wh_ref, cw_ref, y_ref, wbf_ref, ext_ref):
    i = pl.program_id(1)
    tm = x_ref.shape[0]

    @pl.when(i == 0)
    def _():
        wbf_ref[0] = wb_ref[...].astype(BF16)
        wbf_ref[1] = wc_ref[...].astype(BF16)
        wbf_ref[2] = wh_ref[...].astype(BF16)

    @pl.when(i % (SEQ // tm) == 0)
    def _():
        ext_ref[0:8, :] = jnp.zeros((8, ext_ref.shape[1]), F32)

    cw = cw_ref[...]
    x = x_ref[...].astype(BF16)
    gate_b = jnp.dot(x, wbf_ref[0], preferred_element_type=F32)
    gate_c = jnp.dot(x, wbf_ref[1], preferred_element_type=F32)
    h = jnp.dot(x, wbf_ref[2], preferred_element_type=F32)
    u = gate_c * h
    ext_ref[8:8 + tm, :] = u
    u1 = ext_ref[7:7 + tm, :]
    u2 = ext_ref[6:6 + tm, :]
    conv = cw[2:3, :] * u + cw[1:2, :] * u1 + cw[0:1, :] * u2
    y_ref[...] = (gate_b * conv).astype(BF16)
    ext_ref[0:8, :] = ext_ref[tm:tm + 8, :]


def _conv_proj(xb, w_in, conv_w, layer):
    tm, tn = (CONV_TM if xb.dtype == BF16 else CONV_TM // 2), CONV_TN
    nj = D_MODEL // tn
    wspec = lambda off: pl.BlockSpec((None, D_MODEL, tn), lambda j, i: (layer, 0, j + off * nj))
    return pl.pallas_call(
        _conv_proj_kernel,
        out_shape=jax.ShapeDtypeStruct((N_TOK, D_MODEL), BF16),
        grid=(nj, N_TOK // tm),
        in_specs=[
            pl.BlockSpec((tm, D_MODEL), lambda j, i: (i, 0)),
            wspec(0), wspec(1), wspec(2),
            pl.BlockSpec((None, CONV_WIDTH, tn), lambda j, i: (layer, 0, j)),
        ],
        out_specs=pl.BlockSpec((tm, tn), lambda j, i: (i, j)),
        scratch_shapes=[pltpu.VMEM((3, D_MODEL, tn), BF16), pltpu.VMEM((tm + 8, tn), F32)],
        compiler_params=_params(2),
        name="conv_proj",
    )(xb, w_in, w_in, w_in, conv_w)


def _proj_kernel(x_ref, w_ref, c_ref, s1_ref, s2_ref, o_ref, wbf_ref, rot_ref, *maybe_rot2_ref, rope, dilation):
    tm = x_ref.shape[0]
    rot2_ref = maybe_rot2_ref[0] if maybe_rot2_ref else None

    @pl.when(pl.program_id(0) == 0)
    def _():
        wbf_ref[...] = w_ref[...].astype(BF16)

    acc = jnp.dot(x_ref[...], wbf_ref[...], preferred_element_type=F32)
    if rope:
        c = c_ref[...]
        s1 = s1_ref[...]
        s2 = s2_ref[...]
    for hd in range(HEADS_PER_BRANCH):
        sl = slice(hd * HEAD_DIM, (hd + 1) * HEAD_DIM)
        t = acc[:, sl]
        if rope:
            t = t * c + pltpu.roll(t, HEAD_DIM - ROT_DIM // 2, 1) * s1 + pltpu.roll(t, ROT_DIM // 2, 1) * s2
        if dilation == 1:
            o_ref[:, sl] = t.astype(BF16)
            continue
        rot_ref[hd] = t
        for r in range(dilation):
            if dilation <= SPLIT_STRIDE:
                piece = rot_ref.at[hd][pl.ds(r, tm // dilation, stride=dilation), :]
            else:
                q, j = divmod(r, SPLIT_STRIDE)
                if q == 0:
                    rot2_ref[hd, j] = rot_ref.at[hd][pl.ds(j, tm // SPLIT_STRIDE, stride=SPLIT_STRIDE), :]
                piece = rot2_ref.at[hd, j][pl.ds(q, tm // dilation, stride=dilation // SPLIT_STRIDE), :]
            o_ref[:, r * BRANCH_WIDTH + hd * HEAD_DIM:r * BRANCH_WIDTH + (hd + 1) * HEAD_DIM] = piece.astype(BF16)


def _proj(xb, w3d, layer, col, dilation, tables, rope):
    tm = PROJ_TM
    w = BRANCH_WIDTH
    tab = pl.BlockSpec((tm, HEAD_DIM), lambda i: (i % (SEQ // tm), 0))
    scratch = [pltpu.VMEM((D_MODEL, w), BF16), pltpu.VMEM((HEADS_PER_BRANCH, tm, HEAD_DIM), F32)]
    if dilation > SPLIT_STRIDE:
        scratch.append(pltpu.VMEM((HEADS_PER_BRANCH, SPLIT_STRIDE, tm // SPLIT_STRIDE, HEAD_DIM), F32))
    return pl.pallas_call(
        functools.partial(_proj_kernel, rope=rope, dilation=dilation),
        out_shape=jax.ShapeDtypeStruct((N_TOK // dilation, dilation * w), BF16),
        grid=(N_TOK // tm,),
        in_specs=[
            pl.BlockSpec((tm, D_MODEL), lambda i: (i, 0)),
            pl.BlockSpec((None, D_MODEL, w), lambda i: (layer, 0, col)),
            tab, tab, tab,
        ],
        out_specs=pl.BlockSpec((tm // dilation, dilation * w), lambda i: (i, 0)),
        scratch_shapes=scratch,
        compiler_params=_params(1),
        name=("proj_rope" if rope else "proj") + f"_d{dilation}",
    )(xb, w3d, *tables)


def _rope_tables(scale):
    inv_freq = ROPE_THETA ** (-jnp.arange(0, ROT_DIM, 2, dtype=F32) / ROT_DIM)
    ang = jnp.arange(SEQ, dtype=F32)[:, None] * inv_freq[None, :]
    cos, sin = jnp.cos(ang), jnp.sin(ang)
    half = ROT_DIM // 2
    rest = HEAD_DIM - ROT_DIM
    c = jnp.concatenate([cos, cos, jnp.ones((SEQ, rest), F32)], axis=1) * scale
    s1 = jnp.concatenate([-sin, jnp.zeros((SEQ, HEAD_DIM - half), F32)], axis=1) * scale
    s2 = jnp.concatenate([jnp.zeros((SEQ, half), F32), sin, jnp.zeros((SEQ, rest), F32)], axis=1) * scale
    return c, s1, s2


def _attn_kernel(q_ref, kc_ref, kp_ref, vc_ref, vp_ref, o_ref, l_ref):
    qb = q_ref.shape[0]
    qi = lax.broadcasted_iota(jnp.int32, (SPAN, 2 * SPAN), 0)
    kj = lax.broadcasted_iota(jnp.int32, (SPAN, 2 * SPAN), 1)
    dist = SPAN + qi - kj
    band = (dist >= 0) & (dist <= SPAN)
    first_key = jnp.where(pl.program_id(1) == 0, SPAN, 0)
    band_first = band & (kj >= first_key)
    for a in range(qb // SPAN):
        rows = slice(a * SPAN, (a + 1) * SPAN)
        for hd in range(q_ref.shape[1] // HEAD_DIM):
            cols = slice(hd * HEAD_DIM, (hd + 1) * HEAD_DIM)
            q = q_ref[rows, cols]
            if a == 0:
                k_prev, v_prev = kp_ref[:, cols], vp_ref[:, cols]
                mask = band_first
            else:
                prows = slice((a - 1) * SPAN, a * SPAN)
                k_prev, v_prev = kc_ref[prows, cols], vc_ref[prows, cols]
                mask = band
            kk = jnp.concatenate([k_prev, kc_ref[rows, cols]], axis=0)
            vv = jnp.concatenate([v_prev, vc_ref[rows, cols]], axis=0)
            s = lax.dot_general(q, kk, (((1,), (1,)), ((), ())), preferred_element_type=F32)
            s = jnp.where(mask, s, -jnp.inf)
            m = jnp.max(s, axis=-1, keepdims=True)
            p = jnp.exp(s - m)
            den = jnp.sum(p, axis=-1, keepdims=True)
            o = jnp.dot(p.astype(BF16), vv, preferred_element_type=F32) / den
            o_ref[rows, cols] = o.astype(BF16)
            l_ref[rows, cols] = jnp.broadcast_to(m + jnp.log(den), (SPAN, HEAD_DIM))


def _attn_branch(q, k, v, dilation):
    sub = SEQ // dilation
    qb = min(ATTN_QB, sub)
    n_res = min(dilation, ATTN_QB // qb)
    w = n_res * BRANCH_WIDTH
    view = lambda t: t.reshape(BATCH, sub, dilation * BRANCH_WIDTH)
    cur = pl.BlockSpec((None, qb, w), lambda b, n, r: (b, n, r))
    prev = pl.BlockSpec((None, SPAN, w),
                        lambda b, n, r: (b, jnp.maximum(n * (qb // SPAN) - 1, 0), r))
    o, l = pl.pallas_call(
        _attn_kernel,
        out_shape=(jax.ShapeDtypeStruct((BATCH, sub, dilation * BRANCH_WIDTH), BF16),
                   jax.ShapeDtypeStruct((BATCH, sub, dilation * BRANCH_WIDTH), F32)),
        grid=(BATCH, sub // qb, dilation // n_res),
        in_specs=[cur, cur, prev, cur, prev],
        out_specs=(cur, cur),
        compiler_params=_params(3),
        name=f"dilated_attn_d{dilation}",
    )(view(q), view(k), view(k), view(v), view(v))
    width = dilation * BRANCH_WIDTH
    return o.reshape(N_TOK // dilation, width), l.reshape(N_TOK // dilation, width)


def _layer_norm(z, g, b):
    mean = jnp.mean(z, axis=-1, keepdims=True)
    zc = z - mean
    var = jnp.mean(zc * zc, axis=-1, keepdims=True)
    return zc * lax.rsqrt(var + LN_EPS) * g + b


def _route(logits_t, bias_col):
    scores = jax.nn.sigmoid(logits_t)
    sel = scores + bias_col
    sel_r = [sel[e:e + 1, :] for e in range(N_EXPERTS)]
    sc_r = [scores[e:e + 1, :] for e in range(N_EXPERTS)]
    best_g = None
    for g in range(N_GROUPS):
        v = sel_r[g * EXPERTS_PER_GROUP:(g + 1) * EXPERTS_PER_GROUP]
        gs = None
        for a in range(EXPERTS_PER_GROUP):
            for b in range(a + 1, EXPERTS_PER_GROUP):
                ps = v[a] + v[b]
                gs = ps if gs is None else jnp.maximum(gs, ps)
        if best_g is None:
            best_g, gidx = gs, jnp.zeros_like(gs, dtype=jnp.int32)
        else:
            upd = gs > best_g
            best_g = jnp.where(upd, gs, best_g)
            gidx = jnp.where(upd, g, gidx)
    cand, raw = [], []
    for j in range(EXPERTS_PER_GROUP):
        cv, rv = sel_r[j], sc_r[j]
        for g in range(1, N_GROUPS):
            pick = gidx == g
            cv = jnp.where(pick, sel_r[g * EXPERTS_PER_GROUP + j], cv)
            rv = jnp.where(pick, sc_r[g * EXPERTS_PER_GROUP + j], rv)
        cand.append(cv)
        raw.append(rv)

    def argmax4(vals):
        best, idx = vals[0], jnp.zeros_like(gidx)
        for j in range(1, EXPERTS_PER_GROUP):
            upd = vals[j] > best
            best = jnp.where(upd, vals[j], best)
            idx = jnp.where(upd, j, idx)
        return idx

    i1 = argmax4(cand)
    i2 = argmax4([jnp.where(i1 == j, -jnp.inf, cand[j]) for j in range(EXPERTS_PER_GROUP)])
    pick_raw = lambda idx: sum(jnp.where(idx == j, raw[j], 0.0) for j in range(EXPERTS_PER_GROUP))
    g1, g2 = pick_raw(i1), pick_raw(i2)
    tot = g1 + g2
    e1 = gidx * EXPERTS_PER_GROUP + i1
    e2 = gidx * EXPERTS_PER_GROUP + i2
    return e1, e2, g1 / tot, g2 / tot


def _out_ln_route_tail(mix, x_ref, g_ref, b_ref, rw_ref, rb_ref, x1_ref, e_ref, gt_ref):
    z = DEEPNORM_ALPHA * x_ref[...] + mix
    x1 = _layer_norm(z, g_ref[...], b_ref[...])
    x1_ref[...] = x1
    x_hi = x1.astype(BF16)
    x_lo = (x1 - x_hi.astype(F32)).astype(BF16)
    p_hi = jnp.dot(x_hi, rw_ref[...], preferred_element_type=F32)
    p_lo = jnp.dot(x_lo, rw_ref[:, :LANES], preferred_element_type=F32)
    logits = p_hi[:, :LANES] + (p_hi[:, LANES:] + p_lo)
    logits_t = logits.T[0:N_EXPERTS, :]
    e1, e2, g1, g2 = _route(logits_t, rb_ref[...])
    e_ref[0:1, :] = e1
    e_ref[1:2, :] = e2
    gt_ref[0:1, :] = g1
    gt_ref[1:2, :] = g2


def _out_ln_route_a_kernel(a_ref, w_ref, *rest):
    mix = jnp.dot(a_ref[...], w_ref[...], preferred_element_type=F32)
    _out_ln_route_tail(mix, *rest)


def _out_ln_route_b_kernel(o0_ref, o1_ref, o2_ref, l0_ref, l1_ref, l2_ref, w_ref,
                           x_ref, g_ref, b_ref, rw_ref, rb_ref, x1_ref, e_ref, gt_ref,
                           os_ref, ls_ref):
    tm = x_ref.shape[0]
    merged = []
    for hd in range(HEADS_PER_BRANCH):
        for g, (o_ref, l_ref) in enumerate(((o1_ref, l1_ref), (o2_ref, l2_ref))):
            d = BRANCHES[g + 1][1]
            for r in range(d):
                cols = slice(r * BRANCH_WIDTH + hd * HEAD_DIM, r * BRANCH_WIDTH + (hd + 1) * HEAD_DIM)
                os_ref.at[g, hd][pl.ds(r, tm // d, stride=d), :] = o_ref[:, cols].astype(F32)
                ls_ref.at[g, hd][pl.ds(r, tm // d, stride=d), :] = l_ref[:, cols]
        cols = slice(hd * HEAD_DIM, (hd + 1) * HEAD_DIM)
        l0, l1, l2 = l0_ref[:, cols], ls_ref[0, hd], ls_ref[1, hd]
        lm = jnp.maximum(jnp.maximum(l0, l1), l2)
        w0, w1, w2 = jnp.exp(l0 - lm), jnp.exp(l1 - lm), jnp.exp(l2 - lm)
        m = (o0_ref[:, cols].astype(F32) * w0 + os_ref[0, hd] * w1 + os_ref[1, hd] * w2) / (w0 + w1 + w2)
        merged.append(m.astype(BF16))
    mix = jnp.dot(jnp.concatenate(merged, axis=1), w_ref[...], preferred_element_type=F32)
    _out_ln_route_tail(mix, x_ref, g_ref, b_ref, rw_ref, rb_ref, x1_ref, e_ref, gt_ref)


def _out_ln_route(lhs, w_bf, x, ln_g, ln_b, rw_split, rb_col):
    tm = OUT_TM
    kdim = w_bf.shape[0]
    row = lambda width: pl.BlockSpec((tm, width), lambda i: (i, 0))
    full = lambda shape: pl.BlockSpec(shape, lambda i: (0,) * len(shape))
    lane_row = pl.BlockSpec((TOP_K, tm), lambda i: (0, i))
    if len(lhs) == 1:
        body, name, lhs_specs, scratch = _out_ln_route_a_kernel, "out_ln_route_conv", [row(kdim)], []
    else:
        body, name = _out_ln_route_b_kernel, "out_ln_route_attn"
        blocked = [pl.BlockSpec((tm // d, d * BRANCH_WIDTH), lambda i: (i, 0)) for _, d in BRANCHES]
        lhs_specs = blocked + blocked
        scratch = [pltpu.VMEM((N_BRANCH - 1, HEADS_PER_BRANCH, tm, HEAD_DIM), F32)] * 2
    return pl.pallas_call(
        body,
        out_shape=(jax.ShapeDtypeStruct((N_TOK, D_MODEL), F32),
                   jax.ShapeDtypeStruct((TOP_K, N_TOK), jnp.int32),
                   jax.ShapeDtypeStruct((TOP_K, N_TOK), F32)),
        grid=(N_TOK // tm,),
        in_specs=lhs_specs + [
            full((kdim, D_MODEL)), row(D_MODEL), full((1, D_MODEL)), full((1, D_MODEL)),
            full((D_MODEL, 2 * LANES)), full((N_EXPERTS, 1)),
        ],
        out_specs=(row(D_MODEL), lane_row, lane_row),
        scratch_shapes=scratch,
        compiler_params=_params(1),
        name=name,
    )(*lhs, w_bf, x, ln_g.reshape(1, D_MODEL), ln_b.reshape(1, D_MODEL), rw_split, rb_col)


def _build_plan(eidx, gates):
    flat_e = eidx.reshape(-1)
    onehot = (flat_e[:, None] == jnp.arange(N_EXPERTS, dtype=jnp.int32)[None, :]).astype(jnp.int32)
    csum = jnp.cumsum(onehot, axis=0)
    rank = jnp.sum(onehot * csum, axis=1) - 1
    counts = csum[-1]
    nblk_e = ((counts + MOE_BLK - 1) // MOE_BLK).astype(jnp.int32)
    blk_end = jnp.cumsum(nblk_e).astype(jnp.int32)
    blk_start = blk_end - nblk_e
    dest = (blk_start[flat_e] * MOE_BLK + rank).astype(jnp.int32)
    n_valid = blk_end[-1]

    chunks = jnp.full((1,), MOE_CHUNKS, jnp.int32)
    phase_len = jnp.concatenate([chunks, jnp.maximum(nblk_e[:-1], MOE_CHUNKS), nblk_e[-1:]])
    phase_end = jnp.cumsum(phase_len).astype(jnp.int32)
    phase_start = phase_end - phase_len
    step = jnp.arange(MOE_STEPS, dtype=jnp.int32)
    ph = jnp.minimum(jnp.sum((step[:, None] >= phase_end[None, :]).astype(jnp.int32), axis=1), N_EXPERTS)
    off = step - phase_start[ph]
    live = step < phase_end[-1]
    ce = jnp.clip(ph - 1, 0, N_EXPERTS - 1)
    nb_c = jnp.where(ph >= 1, nblk_e[ce], 0)
    comp_on = (live & (ph >= 1) & (off < nb_c)).astype(jnp.int32)
    comp_blk = jnp.clip(jnp.where(ph >= 1, blk_start[ce], 0) + jnp.minimum(off, nb_c - 1), 0, n_valid - 1)
    load_on = (live & (ph < N_EXPERTS) & (off < MOE_CHUNKS)).astype(jnp.int32)
    load_e = jnp.minimum(ph, N_EXPERTS - 1)
    load_c = jnp.where(ph < N_EXPERTS, jnp.minimum(off, MOE_CHUNKS - 1), MOE_CHUNKS - 1)
    sched = (comp_on, comp_blk.astype(jnp.int32), (ph + 1) % 2, load_on, load_e, load_c.astype(jnp.int32), ph % 2)
    return dest[:N_TOK], dest[N_TOK:], blk_end - 1, nblk_e, sched, gates.T


def _dispatch_kernel(d0_ref, d1_ref, last_ref, nblk_ref, x1_ref, xs_hbm, xp_ref, zero_ref, zsem, sem):
    i = pl.program_id(0)
    tm = x1_ref.shape[0]
    blk_rows = MOE_BLK * TOK_PITCH
    _store_token_rows(xp_ref, tm, x1_ref[...])

    @pl.when(i == 0)
    def _():
        zero_ref[...] = jnp.zeros_like(zero_ref)

        def zero_copy(e):
            start = pl.multiple_of(last_ref[e] * blk_rows, blk_rows)
            return pltpu.make_async_copy(zero_ref, xs_hbm.at[pl.ds(start, blk_rows)], zsem)

        for e in range(N_EXPERTS):
            @pl.when(nblk_ref[e] > 0)
            def _():
                zero_copy(e).start()
        for e in range(N_EXPERTS):
            @pl.when(nblk_ref[e] > 0)
            def _():
                zero_copy(e).wait()

    def body(r, carry):
        t = i * tm + r
        src = xp_ref.at[_token_slot(r)]
        for k, d_ref in enumerate((d0_ref, d1_ref)):
            dst = xs_hbm.at[_token_slot(d_ref[t])]
            pltpu.make_async_copy(src, dst, sem).start(priority=k)
        return carry

    lax.fori_loop(0, tm, body, 0, unroll=DMA_UNROLL)
    for _ in range(TOP_K):
        whole = pl.ds(0, tm * TOK_ROWS)
        pltpu.make_async_copy(xp_ref.at[whole], xs_hbm.at[whole], sem).wait()


def _moe_dispatch(xp, dest0, dest1, last_blk, nblk_e):
    tm = DISPATCH_TM
    return pl.pallas_call(
        _dispatch_kernel,
        out_shape=jax.ShapeDtypeStruct((MOE_ROWS * TOK_PITCH, LANES), F32),
        grid_spec=pltpu.PrefetchScalarGridSpec(
            num_scalar_prefetch=4,
            grid=(N_TOK // tm,),
            in_specs=[pl.BlockSpec((tm, D_MODEL), lambda i, *_: (i, 0))],
            out_specs=pl.BlockSpec(memory_space=pl.ANY),
            scratch_shapes=[pltpu.VMEM((tm * TOK_PITCH, LANES), F32),
                            pltpu.VMEM((MOE_BLK * TOK_PITCH, LANES), F32),
                            pltpu.SemaphoreType.DMA, pltpu.SemaphoreType.DMA],
        ),
        compiler_params=_params(1),
        name="moe_dispatch",
    )(dest0, dest1, last_blk, nblk_e, xp)


def _moe_ffn_kernel(con_ref, cblk_ref, cslot_ref, lon_ref, le_ref, lc_ref, lslot_ref,
                    xs_ref, w1_ref, w3_ref, w2_ref, ys_ref, w13_ref, w2b_ref):
    s = pl.program_id(0)
    blk = MOE_BLK
    n_ff = D_FF // LANES

    @pl.when(lon_ref[s] == 1)
    def _():
        slot = lslot_ref[s]
        rows = pl.ds(pl.multiple_of(lc_ref[s] * MOE_UP_ROWS, MOE_UP_ROWS), MOE_UP_ROWS)
        for j in range(n_ff):
            src = slice(j * LANES, (j + 1) * LANES)
            w13_ref[slot, rows, 2 * j * LANES:(2 * j + 1) * LANES] = w1_ref[:, src].astype(BF16)
            w13_ref[slot, rows, (2 * j + 1) * LANES:(2 * j + 2) * LANES] = w3_ref[:, src].astype(BF16)
        rows2 = pl.ds(pl.multiple_of(lc_ref[s] * MOE_DOWN_ROWS, 16), MOE_DOWN_ROWS)
        w2b_ref[slot, rows2, :] = w2_ref[...].astype(BF16)

    @pl.when(con_ref[s] == 1)
    def _():
        slot = cslot_ref[s]
        x = jnp.concatenate([p.astype(BF16) for p in _load_token_rows(xs_ref, blk)], axis=1)
        g = []
        for q in range(0, n_ff, 2):
            npair = min(2, n_ff - q)
            h = jnp.dot(x, w13_ref[slot, :, 2 * q * LANES:2 * (q + npair) * LANES], preferred_element_type=F32)
            for k in range(npair):
                h1, h3 = h[:, 2 * k * LANES:(2 * k + 1) * LANES], h[:, (2 * k + 1) * LANES:(2 * k + 2) * LANES]
                g.append((jax.nn.silu(h1) * h3).astype(BF16))
        y = jnp.dot(jnp.concatenate(g, axis=1), w2b_ref[slot], preferred_element_type=F32)
        _store_token_rows(ys_ref, blk, y)


def _moe_ffn(xs, sched, w1, w3, w2, layer):
    blk = MOE_BLK
    rows = pl.BlockSpec((blk * TOK_PITCH, LANES), lambda s, con, cblk, *_: (cblk[s], 0))
    up = pl.BlockSpec((None, None, MOE_UP_ROWS, D_FF),
                      lambda s, con, cblk, cslot, lon, le, lc, *_: (layer, le[s], lc[s], 0))
    down = pl.BlockSpec((None, None, MOE_DOWN_ROWS, D_MODEL),
                        lambda s, con, cblk, cslot, lon, le, lc, *_: (layer, le[s], lc[s], 0))
    return pl.pallas_call(
        _moe_ffn_kernel,
        out_shape=jax.ShapeDtypeStruct((MOE_ROWS * TOK_PITCH, LANES), F32),
        grid_spec=pltpu.PrefetchScalarGridSpec(
            num_scalar_prefetch=len(sched),
            grid=(MOE_STEPS,),
            in_specs=[rows, up, up, down],
            out_specs=rows,
            scratch_shapes=[pltpu.VMEM((2, D_MODEL, 2 * D_FF), BF16), pltpu.VMEM((2, D_FF, D_MODEL), BF16)],
        ),
        compiler_params=_params(1),
        name="moe_ffn",
    )(*sched, xs, w1, w3, w2)


def _combine_ln_kernel(d0_ref, d1_ref, x1_ref, gt_ref, ys_hbm, g_ref, b_ref, x2_ref, x2b_ref,
                       buf_ref, z_ref, sems):
    i = pl.program_id(0)
    tm = x2_ref.shape[0]

    def issue(tile):
        slot = tile % 2

        def body(r, carry):
            t = tile * tm + r
            for k, d_ref in enumerate((d0_ref, d1_ref)):
                src = ys_hbm.at[_token_slot(d_ref[t])]
                dst = buf_ref.at[slot, k, _token_slot(r)]
                pltpu.make_async_copy(src, dst, sems.at[slot]).start(priority=k)
            return carry

        lax.fori_loop(0, tm, body, 0, unroll=DMA_UNROLL)

    @pl.when(i == 0)
    def _():
        issue(0)

    @pl.when(i + 1 < pl.num_programs(0))
    def _():
        issue(i + 1)

    slot = i % 2
    for k in range(TOP_K):
        whole = pl.ds(0, tm * TOK_ROWS)
        pltpu.make_async_copy(ys_hbm.at[whole], buf_ref.at[slot, k, whole], sems.at[slot]).wait()
    gt = gt_ref[...]
    g0, g1 = gt[:, 0:1], gt[:, 1:2]
    chunks = zip(_load_token_rows(buf_ref.at[slot, 0], tm), _load_token_rows(buf_ref.at[slot, 1], tm))
    for c, (y0, y1) in enumerate(chunks):
        z_ref[:, c * LANES:(c + 1) * LANES] = y0 * g0 + y1 * g1
    x2 = _layer_norm(DEEPNORM_ALPHA * x1_ref[...] + z_ref[...], g_ref[...], b_ref[...])
    x2_ref[...] = x2
    x2b_ref[...] = x2.astype(BF16)


def _combine_ln(x1, ys, dest0, dest1, gates_t, ln_g, ln_b):
    tm = COMB_TM
    row = lambda width: pl.BlockSpec((tm, width), lambda i, d0, d1: (i, 0))
    vec = pl.BlockSpec((1, D_MODEL), lambda i, d0, d1: (0, 0))
    return pl.pallas_call(
        _combine_ln_kernel,
        out_shape=(jax.ShapeDtypeStruct((N_TOK, D_MODEL), F32),
                   jax.ShapeDtypeStruct((N_TOK, D_MODEL), BF16)),
        grid_spec=pltpu.PrefetchScalarGridSpec(
            num_scalar_prefetch=2,
            grid=(N_TOK // tm,),
            in_specs=[row(D_MODEL), row(TOP_K), pl.BlockSpec(memory_space=pl.ANY), vec, vec],
            out_specs=(row(D_MODEL), row(D_MODEL)),
            scratch_shapes=[pltpu.VMEM((2, TOP_K, tm * TOK_PITCH, LANES), F32),
                            pltpu.VMEM((tm, D_MODEL), F32),
                            pltpu.SemaphoreType.DMA((2,))],
        ),
        compiler_params=_params(1),
        name="combine_ln",
    )(dest0, dest1, x1, gates_t, ys, ln_g.reshape(1, D_MODEL), ln_b.reshape(1, D_MODEL))


def kernel(x, a_w_in, a_conv_w, a_w_out, kv_w, b_w_q, b_w_o, router_w, router_bias,
           moe_w1, moe_w3, moe_w2, ln1_g, ln1_b, ln2_g, ln2_b):
    x = x.reshape(N_TOK, D_MODEL)
    xb = x
    rw_pad = jnp.pad(router_w.astype(F32), ((0, 0), (0, LANES - N_EXPERTS)))
    rw_hi = rw_pad.astype(BF16)
    rw_lo = (rw_pad - rw_hi.astype(F32)).astype(BF16)
    rw_split = jnp.concatenate([rw_hi, rw_lo], axis=1)
    rb_col = router_bias.astype(F32).reshape(N_EXPERTS, 1)
    k_tables = _rope_tables(1.0)
    q_tables = _rope_tables(1.0 / math.sqrt(HEAD_DIM))
    dils = [d for _, d in BRANCHES]
    k_sh = v_sh = None
    for i in range(DEPTH):
        if i < N_A_LAYERS:
            y = _conv_proj(xb, a_w_in, a_conv_w, i)
            lhs, w_out = [y], a_w_out[i].astype(BF16)
        else:
            j = i - N_A_LAYERS
            if k_sh is None:
                kv3 = kv_w[None]
                k_sh = [_proj(xb, kv3, 0, g, dils[g], k_tables, rope=True) for g in range(N_BRANCH)]
                v_sh = [_proj(xb, kv3, 0, N_BRANCH + g, dils[g], k_tables, rope=False) for g in range(N_BRANCH)]
            q = [_proj(xb, b_w_q, j, g, dils[g], q_tables, rope=True) for g in range(N_BRANCH)]
            outs = [_attn_branch(q[g], k_sh[g], v_sh[g], dils[g]) for g in range(N_BRANCH)]
            lhs, w_out = [o for o, _ in outs] + [l for _, l in outs], b_w_o[j].astype(BF16)
        x1, eidx, gates = _out_ln_route(lhs, w_out, x, ln1_g[i], ln1_b[i], rw_split, rb_col)
        dest0, dest1, last_blk, nblk_e, sched, gates_t = _build_plan(eidx, gates)
        xs = _moe_dispatch(x1, dest0, dest1, last_blk, nblk_e)
        ys = _moe_ffn(xs, sched, moe_w1, moe_w3, moe_w2, i)
        x, xb = _combine_ln(x1, ys, dest0, dest1, gates_t, ln2_g[i], ln2_b[i])
    return x.reshape(BATCH, SEQ, D_MODEL)
```
